```python
import math
import jax, jax.numpy as jnp
from jax import lax
import numpy as np

D_MODEL = 4096
BATCH = 4
SEQ = 4096
DEPTH = 2

HG_HEADS = 8
HG_DK = 128
HG_DV = 128
HG_WIDTH = HG_HEADS * HG_DV
GDN_HEADS = 8
GDN_DK = 128
GDN_DV = 128
GDN_QK_WIDTH = GDN_HEADS * GDN_DK
GDN_WIDTH = GDN_HEADS * GDN_DV
GDN_CONV = 4
GDN_CONV_WIDTH = 2 * GDN_QK_WIDTH + GDN_WIDTH
SWA_Q_HEADS = 16
SWA_KV_HEADS = 2
SWA_HEAD_DIM = 64
SWA_WIDTH = SWA_Q_HEADS * SWA_HEAD_DIM
SWA_KV_WIDTH = SWA_KV_HEADS * SWA_HEAD_DIM
SWA_WINDOW = 128
SWA_BLOCK = 128
CHUNK = 64
N_BRANCH = 3
BRANCH_WIDTH = 1024
IN_SIZES = (HG_HEADS * HG_DK, HG_HEADS * HG_DK, HG_WIDTH, HG_WIDTH,
            GDN_CONV_WIDTH, GDN_WIDTH, GDN_HEADS, GDN_HEADS,
            SWA_WIDTH, SWA_KV_WIDTH, SWA_KV_WIDTH,
            N_BRANCH * D_MODEL)
IN_WIDTH = sum(IN_SIZES)
D_FF = -(-8 * D_MODEL // (3 * 256)) * 256
DEEPNORM_ALPHA = (2 * DEPTH) ** 0.25
DEEPNORM_BETA = (8 * DEPTH) ** -0.25
LN_EPS = 1e-5
RMS_EPS = 1e-6
L2_EPS = 1e-6

kernel_name = "hybrid_hgrn2_gdn_swa_gated_deepnorm"

F32 = jnp.float32


def layer_norm(x, g, b):
    xf = x.astype(F32)
    mu = jnp.mean(xf, -1, keepdims=True)
    var = jnp.mean(jnp.square(xf - mu), -1, keepdims=True)
    return ((xf - mu) * lax.rsqrt(var + LN_EPS) * g.astype(F32) + b.astype(F32)).astype(x.dtype)


def rms_norm(x, g):
    xf = x.astype(F32)
    return xf * lax.rsqrt(jnp.mean(jnp.square(xf), -1, keepdims=True) + RMS_EPS) * g.astype(F32)


def l2_normalize(x):
    return x * lax.rsqrt(jnp.sum(jnp.square(x), -1, keepdims=True) + L2_EPS)


def alibi_slopes(n_heads):
    return jnp.exp2(-8.0 * jnp.arange(1, n_heads + 1, dtype=F32) / n_heads)


def split_projection(proj):
    idx, acc = [], 0
    for s in IN_SIZES[:-1]:
        acc += s
        idx.append(acc)
    return jnp.split(proj, idx, axis=-1)


def to_chunks(t):
    B, T, H, d = t.shape
    return t.reshape(B, T // CHUNK, CHUNK, H, d).transpose(1, 0, 3, 2, 4)


def from_chunks(t):
    n, B, H, C, d = t.shape
    return t.transpose(1, 0, 3, 2, 4).reshape(B, n * C, H, d)


def to_chunks_scalar(t):
    B, T, H = t.shape
    return t.reshape(B, T // CHUNK, CHUNK, H).transpose(1, 0, 3, 2)


def hgrn2_chunked(q, k, v, log_f):
    B, T, H, DK = q.shape
    DV = v.shape[-1]
    qc, kc, vc = to_chunks(q.astype(F32)), to_chunks(k.astype(F32)), to_chunks(v.astype(F32))
    Gc = jnp.cumsum(to_chunks(log_f.astype(F32)), axis=-2)
    causal = jnp.tril(jnp.ones((CHUNK, CHUNK), bool))

    def step(S, inp):
        qi, ki, vi, Gi = inp
        diff = Gi[..., :, None, :] - Gi[..., None, :, :]
        decay = jnp.exp(jnp.where(causal[:, :, None], diff, -jnp.inf))
        A = jnp.einsum('bhijc,bhjc->bhij', qi[..., :, None, :] * decay, ki)
        o = (jnp.einsum('bhij,bhjv->bhiv', A, vi)
             + jnp.einsum('bhic,bhcv->bhiv', qi * jnp.exp(Gi), S))
        G_last = Gi[..., -1:, :]
        S = (S * jnp.exp(G_last)[..., 0, :, None]
             + jnp.einsum('bhjc,bhjv->bhcv', ki * jnp.exp(G_last - Gi), vi))
        return S, o

    S0 = jnp.zeros((B, H, DK, DV), F32)
    _, out = lax.scan(step, S0, (qc, kc, vc, Gc))
    return from_chunks(out)


def gated_delta_chunked(q, k, v, g, beta):
    B, T, H, DK = q.shape
    DV = v.shape[-1]
    qc, kc, vc = to_chunks(q.astype(F32)), to_chunks(k.astype(F32)), to_chunks(v.astype(F32))
    bc = to_chunks_scalar(beta.astype(F32))
    Gc = jnp.cumsum(to_chunks_scalar(g.astype(F32)), axis=-1)
    causal = jnp.tril(jnp.ones((CHUNK, CHUNK), bool))
    strict = jnp.tril(jnp.ones((CHUNK, CHUNK), bool), -1)
    decay = jnp.exp(jnp.where(causal, Gc[..., :, None] - Gc[..., None, :], -jnp.inf))
    kb = kc * bc[..., None]
    L = jnp.where(strict, jnp.einsum('nbhic,nbhjc->nbhij', kb, kc) * decay, 0.0)
    rhs = jnp.concatenate([vc * bc[..., None], kb * jnp.exp(Gc)[..., None]], axis=-1)
    a_mat = jnp.broadcast_to(jnp.eye(CHUNK, dtype=F32), L.shape) + L
    sol = lax.linalg.triangular_solve(a_mat, rhs, left_side=True, lower=True, unit_diagonal=True)
    u, w = sol[..., :DV], sol[..., DV:]
    qk = jnp.where(causal, jnp.einsum('nbhic,nbhjc->nbhij', qc, kc) * decay, 0.0)

    def step(S, inp):
        qi, ki, ui, wi, qki, Gi = inp
        v_new = ui - jnp.einsum('bhic,bhcv->bhiv', wi, S)
        o = (jnp.einsum('bhic,bhcv->bhiv', qi * jnp.exp(Gi)[..., None], S)
             + jnp.einsum('bhij,bhjv->bhiv', qki, v_new))
        G_last = Gi[..., -1:]
        S = (S * jnp.exp(G_last)[..., None]
             + jnp.einsum('bhjc,bhjv->bhcv', ki * jnp.exp(G_last - Gi)[..., None], v_new))
        return S, o

    S0 = jnp.zeros((B, H, DK, DV), F32)
    _, out = lax.scan(step, S0, (qc, kc, u, w, qk, Gc))
    return from_chunks(out)


def causal_depthwise_conv(x, w):
    K, C = w.shape
    return lax.conv_general_dilated(x, w[:, None, :].astype(x.dtype), window_strides=(1,),
                                    padding=[(K - 1, 0)], dimension_numbers=('NWC', 'WIO', 'NWC'),
                                    feature_group_count=C)


def sliding_window_attention(q, k, v, sinks):
    B, T, HQ, DH = q.shape
    HKV = k.shape[2]
    G = HQ // HKV
    nb = T // SWA_BLOCK
    qb = q.astype(F32).reshape(B, nb, SWA_BLOCK, HKV, G, DH)
    kb = k.astype(F32).reshape(B, nb, SWA_BLOCK, HKV, DH)
    vb = v.astype(F32).reshape(B, nb, SWA_BLOCK, HKV, DH)
    pad = jnp.zeros_like(kb[:, :1])
    k2 = jnp.concatenate([jnp.concatenate([pad, kb[:, :-1]], 1), kb], axis=2)
    v2 = jnp.concatenate([jnp.concatenate([pad, vb[:, :-1]], 1), vb], axis=2)
    s = jnp.einsum('bnqhgd,bnkhd->bnhgqk', qb, k2) * (DH ** -0.5)
    qi = jnp.arange(SWA_BLOCK)[:, None]
    kj = jnp.arange(2 * SWA_BLOCK)[None, :]
    dist = qi - kj + SWA_BLOCK
    key_pos = jnp.arange(nb)[:, None] * SWA_BLOCK - SWA_BLOCK + jnp.arange(2 * SWA_BLOCK)[None, :]
    valid = ((dist >= 0) & (dist < SWA_WINDOW))[None] & (key_pos >= 0)[:, None, :]
    slopes = alibi_slopes(HQ).reshape(HKV, G)
    s = s - slopes[:, :, None, None] * dist.astype(F32)
    s = jnp.where(valid[None, :, None, None], s, -jnp.inf)
    sink = sinks.astype(F32).reshape(HKV, G)[None, None, :, :, None, None]
    m = jnp.maximum(jnp.max(s, -1, keepdims=True), sink)
    p = jnp.exp(s - m)
    p = p / (jnp.sum(p, -1, keepdims=True) + jnp.exp(sink - m))
    o = jnp.einsum('bnhgqk,bnkhd->bnqhgd', p, v2)
    return o.reshape(B, T, HQ * DH)


def hybrid_mixer(x, lb, w_in, conv_w, a_log, dt_bias, hg_norm_g, gdn_norm_g, sinks, w_branch, w_out):
    B, T, _ = x.shape
    (hg_q, hg_f, hg_i, hg_g, gdn_qkv, gdn_z, gdn_a, gdn_b,
     swa_q, swa_k, swa_v, gate_logits) = split_projection(x @ w_in)

    def heads(t, h):
        return t.reshape(B, T, h, -1)

    zf = hg_f.astype(F32)
    log_f = jnp.logaddexp(jnp.log(lb), jnp.log1p(-lb) + jax.nn.log_sigmoid(zf))
    hg_k = (1.0 - lb) * jax.nn.sigmoid(-zf)
    o_a = hgrn2_chunked(heads(jax.nn.silu(hg_q.astype(F32)), HG_HEADS), heads(hg_k, HG_HEADS),
                        heads(hg_i, HG_HEADS), heads(log_f, HG_HEADS))
    o_a = (rms_norm(o_a, hg_norm_g) * heads(jax.nn.silu(hg_g.astype(F32)), HG_HEADS)).reshape(B, T, HG_WIDTH)

    qkv = jax.nn.silu(causal_depthwise_conv(gdn_qkv, conv_w).astype(F32))
    g_q, g_k, g_v = jnp.split(qkv, [GDN_QK_WIDTH, 2 * GDN_QK_WIDTH], axis=-1)
    g_q = l2_normalize(heads(g_q, GDN_HEADS)) * (GDN_DK ** -0.5)
    g_k = l2_normalize(heads(g_k, GDN_HEADS))
    beta = jax.nn.sigmoid(gdn_b.astype(F32))
    g_log = -jnp.exp(a_log.astype(F32)) * jax.nn.softplus(gdn_a.astype(F32) + dt_bias.astype(F32))
    o_b = gated_delta_chunked(g_q, g_k, heads(g_v, GDN_HEADS), g_log, beta)
    o_b = (rms_norm(o_b, gdn_norm_g) * heads(jax.nn.silu(gdn_z.astype(F32)), GDN_HEADS)).reshape(B, T, GDN_WIDTH)

    o_c = sliding_window_attention(heads(swa_q, SWA_Q_HEADS), heads(swa_k, SWA_KV_HEADS),
                                   heads(swa_v, SWA_KV_HEADS), sinks)

    gates = jax.nn.sigmoid(gate_logits.astype(F32)).reshape(B, T, N_BRANCH, D_MODEL)
    branches = (o_a, o_b, o_c)
    merged = gates[:, :, 0] * (branches[0].astype(x.dtype) @ w_branch[0])
    for b_idx in range(1, N_BRANCH):
        merged = merged + gates[:, :, b_idx] * (branches[b_idx].astype(x.dtype) @ w_branch[b_idx])
    return merged.astype(x.dtype) @ w_out


def swiglu(x, w_gate_up, w_down):
    gate, up = jnp.split(x @ w_gate_up, 2, axis=-1)
    return (jax.nn.silu(gate) * up) @ w_down


def setup_inputs(seed: int = 0) -> dict:
    key = jax.random.key(seed)
    ks = jax.random.split(key, 20)

    def nrm(k, shape, scale):
        return jax.random.normal(k, shape, F32) * scale

    x = nrm(ks[0], (BATCH, SEQ, D_MODEL), 1.0)
    ln_in_g = 1.0 + nrm(ks[1], (D_MODEL,), 0.02)
    ln_in_b = nrm(ks[2], (D_MODEL,), 0.02)
    hg_lb_logits = nrm(ks[3], (DEPTH, HG_HEADS * HG_DK), 0.5)
    w_in = nrm(ks[4], (DEPTH, D_MODEL, IN_WIDTH), D_MODEL ** -0.5)
    gdn_conv_w = nrm(ks[5], (DEPTH, GDN_CONV, GDN_CONV_WIDTH), GDN_CONV ** -0.5)
    gdn_a_log = jnp.log(jax.random.uniform(ks[6], (DEPTH, GDN_HEADS), F32, 1.0, 16.0))
    dt = jnp.exp(jax.random.uniform(ks[7], (DEPTH, GDN_HEADS), F32, math.log(1e-3), math.log(1e-1)))
    gdn_dt_bias = dt + jnp.log(-jnp.expm1(-dt))
    hg_norm_g = 1.0 + nrm(ks[8], (DEPTH, HG_DV), 0.02)
    gdn_norm_g = 1.0 + nrm(ks[9], (DEPTH, GDN_DV), 0.02)
    swa_sinks = nrm(ks[10], (DEPTH, SWA_Q_HEADS), 0.5)
    w_branch = nrm(ks[11], (DEPTH, N_BRANCH, BRANCH_WIDTH, D_MODEL), BRANCH_WIDTH ** -0.5)
    w_out = nrm(ks[12], (DEPTH, D_MODEL, D_MODEL), D_MODEL ** -0.5 * DEEPNORM_BETA)
    ln1_g = 1.0 + nrm(ks[13], (DEPTH, D_MODEL), 0.02)
    ln1_b = nrm(ks[14], (DEPTH, D_MODEL), 0.02)
    w_gate_up = nrm(ks[15], (DEPTH, D_MODEL, 2 * D_FF), D_MODEL ** -0.5)
    w_down = nrm(ks[16], (DEPTH, D_FF, D_MODEL), D_FF ** -0.5 * DEEPNORM_BETA)
    ln2_g = 1.0 + nrm(ks[17], (DEPTH, D_MODEL), 0.02)
    ln2_b = nrm(ks[18], (DEPTH, D_MODEL), 0.02)
    return {"x": x, "ln_in_g": ln_in_g, "ln_in_b": ln_in_b, "hg_lb_logits": hg_lb_logits,
            "w_in": w_in, "gdn_conv_w": gdn_conv_w, "gdn_a_log": gdn_a_log, "gdn_dt_bias": gdn_dt_bias,
            "hg_norm_g": hg_norm_g, "gdn_norm_g": gdn_norm_g, "swa_sinks": swa_sinks,
            "w_branch": w_branch, "w_out": w_out, "ln1_g": ln1_g, "ln1_b": ln1_b,
            "w_gate_up": w_gate_up, "w_down": w_down, "ln2_g": ln2_g, "ln2_b": ln2_b}


def reference(x, ln_in_g, ln_in_b, hg_lb_logits, w_in, gdn_conv_w, gdn_a_log, gdn_dt_bias,
              hg_norm_g, gdn_norm_g, swa_sinks, w_branch, w_out, ln1_g, ln1_b,
              w_gate_up, w_down, ln2_g, ln2_b):
    lb_all = jnp.cumsum(jax.nn.softmax(hg_lb_logits.astype(F32), axis=0), axis=0)
    lb_all = lb_all - lb_all[0]
    h = layer_norm(x, ln_in_g, ln_in_b)
    for l in range(DEPTH):
        mix = hybrid_mixer(h, lb_all[l], w_in[l], gdn_conv_w[l], gdn_a_log[l], gdn_dt_bias[l],
                           hg_norm_g[l], gdn_norm_g[l], swa_sinks[l], w_branch[l], w_out[l])
        h = layer_norm(DEEPNORM_ALPHA * h + mix, ln1_g[l], ln1_b[l])
        h = layer_norm(DEEPNORM_ALPHA * h + swiglu(h, w_gate_up[l], w_down[l]), ln2_g[l], ln2_b[l])
    return h
```

```python
import functools
import math

import jax
import jax.numpy as jnp
import numpy as np
from jax import lax
from jax.experimental import pallas as pl
from jax.experimental.pallas import tpu as pltpu

F32 = jnp.float32
BF16 = jnp.bfloat16

N_HEADS = 8
HEAD_DIM = 128
BRANCH_WIDTH = N_HEADS * HEAD_DIM
GDN_CONV = 4
SWA_Q_HEADS = 16
SWA_KV_HEADS = 2
SWA_HEAD_DIM = 64
SWA_BLOCK = 128
CHUNK = 64
SUB = 16
DEPTH = 2
DEEPNORM_ALPHA = (2 * DEPTH) ** 0.25
LN_EPS = 1e-5
RMS_EPS = 1e-6
L2_EPS = 1e-6

COL_HG_Q, COL_HG_F, COL_HG_I, COL_HG_G = 0, 1024, 2048, 3072
COL_GDN_QKV, COL_GDN_Z = 4096, 7168
COL_SWA_Q, COL_SWA_K, COL_SWA_V = 8192, 9216, 9344
COL_AB = 9472
COL_GATES = 10240
LANE = 128

VMEM_LIMIT = 56 * 1024 * 1024


def _cparams(sem):
    return pltpu.CompilerParams(dimension_semantics=sem, vmem_limit_bytes=VMEM_LIMIT)


def _pick(n, cands):
    for c in cands:
        if n % c == 0:
            return c
    raise ValueError(f"no tile for {n} in {cands}")


def _sigmoid(x):
    return 1.0 / (1.0 + jnp.exp(-x))


def _silu(x):
    return x * _sigmoid(x)


def _log_sigmoid(x):
    return jnp.minimum(x, 0.0) - jnp.log1p(jnp.exp(-jnp.abs(x)))


def _softplus(x):
    return jnp.maximum(x, 0.0) + jnp.log1p(jnp.exp(-jnp.abs(x)))


def _logaddexp(a, b):
    return jnp.maximum(a, b) + jnp.log1p(jnp.exp(-jnp.abs(a - b)))


NN = (((1,), (0,)), ((), ()))
NT = (((1,), (1,)), ((), ()))
TN = (((0,), (0,)), ((), ()))


def _dot(a, b, dims=NN):
    return lax.dot_general(a.astype(BF16), b.astype(BF16), dims, preferred_element_type=F32)


def _split2(x):
    hi = x.astype(BF16)
    lo = (x - hi.astype(F32)).astype(BF16)
    return hi, lo


def _dot3(a, b, dims=NN):
    ah, al = _split2(a)
    bh, bl = _split2(b)
    dg = functools.partial(lax.dot_general, dimension_numbers=dims, preferred_element_type=F32)
    return dg(ah, bh) + (dg(ah, bl) + dg(al, bh))


def _dot_exact_lhs(m_bf16, x, dims=NN):
    hi = x.astype(BF16)
    r1 = x - hi.astype(F32)
    mid = r1.astype(BF16)
    lo = (r1 - mid.astype(F32)).astype(BF16)
    dg = functools.partial(lax.dot_general, dimension_numbers=dims, preferred_element_type=F32)
    return dg(m_bf16, hi) + (dg(m_bf16, mid) + dg(m_bf16, lo))


def _dot_exact_rhs(x, m_bf16, dims=NN):
    hi = x.astype(BF16)
    r1 = x - hi.astype(F32)
    mid = r1.astype(BF16)
    lo = (r1 - mid.astype(F32)).astype(BF16)
    dg = functools.partial(lax.dot_general, dimension_numbers=dims, preferred_element_type=F32)
    return dg(hi, m_bf16) + (dg(mid, m_bf16) + dg(lo, m_bf16))


def _block_tri(n, lower):
    r = lax.broadcasted_iota(jnp.int32, (n, n), 0)
    c = lax.broadcasted_iota(jnp.int32, (n, n), 1)
    same = (r // CHUNK) == (c // CHUNK)
    tri = (r >= c) if lower else (r <= c)
    return jnp.where(same & tri, 1.0, 0.0).astype(BF16)


def _ln_core(x, g, b):
    mu = jnp.mean(x, axis=-1, keepdims=True)
    xc = x - mu
    var = jnp.mean(xc * xc, axis=-1, keepdims=True)
    return xc * lax.rsqrt(var + LN_EPS) * g + b


def _ln_kernel(x_ref, g_ref, b_ref, o32_ref, o16_ref):
    y = _ln_core(x_ref[...], g_ref[...], b_ref[...])
    o32_ref[...] = y
    o16_ref[...] = y.astype(BF16)


def _ln_res_kernel(h_ref, y_ref, g_ref, b_ref, o32_ref, o16_ref):
    y = _ln_core(DEEPNORM_ALPHA * h_ref[...] + y_ref[...], g_ref[...], b_ref[...])
    o32_ref[...] = y
    o16_ref[...] = y.astype(BF16)


def _layer_norm(x, g, b, res=None):
    M, D = x.shape
    tm = _pick(M, (256, 128, 64, 32, 16))
    row = pl.BlockSpec((tm, D), lambda i: (i, 0))
    vec = pl.BlockSpec((1, D), lambda i: (0, 0))
    g2, b2 = g.reshape(1, D), b.reshape(1, D)
    out_shape = (jax.ShapeDtypeStruct((M, D), F32), jax.ShapeDtypeStruct((M, D), BF16))
    if res is None:
        return pl.pallas_call(_ln_kernel, grid=(M // tm,), in_specs=[row, vec, vec],
                              out_specs=(row, row), out_shape=out_shape,
                              compiler_params=_cparams(("parallel",)), name="layer_norm")(x, g2, b2)
    return pl.pallas_call(_ln_res_kernel, grid=(M // tm,), in_specs=[row, row, vec, vec],
                          out_specs=(row, row), out_shape=out_shape,
                          compiler_params=_cparams(("parallel",)), name="layer_norm_res")(res, x, g2, b2)


def _mm_kernel(a_ref, w_ref, o_ref):
    o_ref[...] = jnp.dot(a_ref[...], w_ref[...], preferred_element_type=F32).astype(o_ref.dtype)


def _matmul(a, w, out_dtype, name):
    M, K = a.shape
    N = w.shape[1]
    tm = _pick(M, (1024, 512, 256, 128))
    tn = _pick(N, (1024, 512, 256, 128))
    if K > 8192:
        tm, tn = min(tm, 512), min(tn, 512)
    return pl.pallas_call(
        _mm_kernel, grid=(M // tm, N // tn),
        in_specs=[pl.BlockSpec((tm, K), lambda i, j: (i, 0)), pl.BlockSpec((K, tn), lambda i, j: (0, j))],
        out_specs=pl.BlockSpec((tm, tn), lambda i, j: (i, j)),
        out_shape=jax.ShapeDtypeStruct((M, N), out_dtype),
        compiler_params=_cparams(("parallel", "parallel")), name=name)(a, w)


def _glu_kernel(a_ref, wg_ref, wu_ref, o_ref):
    a = a_ref[...]
    g = jnp.dot(a, wg_ref[...], preferred_element_type=F32)
    u = jnp.dot(a, wu_ref[...], preferred_element_type=F32)
    o_ref[...] = (_silu(g) * u).astype(o_ref.dtype)


def _glu_matmul(a, w_gate_up):
    M, K = a.shape
    F = w_gate_up.shape[1] // 2
    tm = _pick(M, (1024, 512, 256, 128))
    tn = _pick(F, (512, 256, 128))
    nf = F // tn
    return pl.pallas_call(
        _glu_kernel, grid=(M // tm, nf),
        in_specs=[pl.BlockSpec((tm, K), lambda i, j: (i, 0)),
                  pl.BlockSpec((K, tn), lambda i, j: (0, j)),
                  pl.BlockSpec((K, tn), lambda i, j: (0, j + nf))],
        out_specs=pl.BlockSpec((tm, tn), lambda i, j: (i, j)),
        out_shape=jax.ShapeDtypeStruct((M, F), BF16),
        compiler_params=_cparams(("parallel", "parallel")), name="ffn_gate_up")(a, w_gate_up, w_gate_up)


def _merge_kernel(oa_ref, ob_ref, oc_ref, wb_ref, g0_ref, g1_ref, g2_ref, o_ref):
    acc = _sigmoid(g0_ref[...]) * jnp.dot(oa_ref[...], wb_ref[0], preferred_element_type=F32)
    acc = acc + _sigmoid(g1_ref[...]) * jnp.dot(ob_ref[...], wb_ref[1], preferred_element_type=F32)
    acc = acc + _sigmoid(g2_ref[...]) * jnp.dot(oc_ref[...], wb_ref[2], preferred_element_type=F32)
    o_ref[...] = acc.astype(o_ref.dtype)


def _merge(o_a, o_b, o_c, w_branch, proj, d_model):
    M = o_a.shape[0]
    tm = _pick(M, (512, 256, 128))
    tn = _pick(d_model, (512, 256, 128))
    g_base = COL_GATES // tn
    g_step = d_model // tn
    o_spec = pl.BlockSpec((tm, BRANCH_WIDTH), lambda i, j: (i, 0))
    g_specs = [pl.BlockSpec((tm, tn), functools.partial(lambda i, j, b: (i, g_base + b * g_step + j), b=b))
               for b in range(3)]
    return pl.pallas_call(
        _merge_kernel, grid=(M // tm, d_model // tn),
        in_specs=[o_spec, o_spec, o_spec,
                  pl.BlockSpec((3, BRANCH_WIDTH, tn), lambda i, j: (0, 0, j))] + g_specs,
        out_specs=pl.BlockSpec((tm, tn), lambda i, j: (i, j)),
        out_shape=jax.ShapeDtypeStruct((M, d_model), BF16),
        compiler_params=_cparams(("parallel", "parallel")), name="branch_merge")(
            o_a, o_b, o_c, w_branch, proj, proj, proj)


def _hgrn2_kernel(q_ref, f_ref, i_ref, g_ref, lb_ref, ng_ref, e_ref, o_ref, st_ref, *, tb):
    @pl.when(pl.program_id(2) == 0)
    def _():
        st_ref[...] = jnp.zeros_like(st_ref)

    lb = lb_ref[...]
    log_lb = jnp.log(lb)
    log1m_lb = jnp.log1p(-lb)
    z = f_ref[...]
    log_f = _logaddexp(log_lb, log1m_lb + _log_sigmoid(z))
    k_all = (1.0 - lb) * _sigmoid(-z)
    q_all = _silu(q_ref[...])
    v_all = i_ref[...]
    g_all = _dot_exact_lhs(_block_tri(tb, True), log_f)

    row16 = lax.broadcasted_iota(jnp.int32, (SUB, HEAD_DIM), 0)
    row64 = lax.broadcasted_iota(jnp.int32, (CHUNK, HEAD_DIM), 0)
    r_blk = lax.broadcasted_iota(jnp.int32, (CHUNK, CHUNK), 0) // SUB
    c_blk = lax.broadcasted_iota(jnp.int32, (CHUNK, CHUNK), 1) // SUB
    e_mat = e_ref[...]
    n_sub = CHUNK // SUB
    neg_inf = -jnp.inf

    st = st_ref[...]
    outs = []
    for c in range(tb // CHUNK):
        sl = slice(c * CHUNK, (c + 1) * CHUNK)
        G, q, k, v = g_all[sl], q_all[sl], k_all[sl], v_all[sl]
        pcs = []
        for I in range(n_sub):
            s0 = I * SUB
            GI, qI = G[s0:s0 + SUB], q[s0:s0 + SUB]
            cols = []
            for j in range(SUB):
                r = s0 + j
                d = jnp.where(row16 >= j, GI - G[r:r + 1], neg_inf)
                cols.append((qI * jnp.exp(d) * k[r:r + 1]).astype(BF16))
            pcs.append(jnp.concatenate(cols, axis=1))
        pcat = jnp.concatenate(pcs, axis=0)
        dfull = jnp.dot(pcat, e_mat, preferred_element_type=F32)
        dfull = jnp.where(r_blk == c_blk, dfull, 0.0)
        a_rows = [dfull[0:SUB]]
        for I in range(1, n_sub):
            s0 = I * SUB
            gb = G[s0 - 1:s0]
            qt = q[s0:s0 + SUB] * jnp.exp(G[s0:s0 + SUB] - gb)
            kx = k * jnp.exp(jnp.where(row64 < s0, gb - G, neg_inf))
            a_rows.append(_dot(qt, kx, NT) + dfull[s0:s0 + SUB])
        a_mat = jnp.concatenate(a_rows, axis=0)
        o = _dot(a_mat, v) + _dot(q * jnp.exp(G), st, NT)
        g_last = G[CHUNK - 1:CHUNK]
        st = st * jnp.exp(g_last) + _dot(v, k * jnp.exp(g_last - G), TN)
        outs.append(o)
    st_ref[...] = st
    o = jnp.concatenate(outs, axis=0) if len(outs) > 1 else outs[0]
    o = o * lax.rsqrt(jnp.mean(o * o, axis=-1, keepdims=True) + RMS_EPS) * ng_ref[...]
    o_ref[...] = (o * _silu(g_ref[...])).astype(o_ref.dtype)


def _hgrn2_emat():
    e = np.zeros((SUB * HEAD_DIM, CHUNK), np.float32)
    for j in range(SUB):
        for I in range(CHUNK // SUB):
            e[j * HEAD_DIM:(j + 1) * HEAD_DIM, I * SUB + j] = 1.0
    return jnp.asarray(e, BF16)


def _hgrn2(proj, lb, norm_g, B, T):
    tb = _pick(T, (256, 128, 64))
    nt = T // tb
    cb = lambda base: pl.BlockSpec((tb, HEAD_DIM), lambda b, h, t, base=base: (b * nt + t, base + h))
    return pl.pallas_call(
        functools.partial(_hgrn2_kernel, tb=tb), grid=(B, N_HEADS, nt),
        in_specs=[cb(COL_HG_Q // LANE), cb(COL_HG_F // LANE), cb(COL_HG_I // LANE), cb(COL_HG_G // LANE),
                  pl.BlockSpec((1, HEAD_DIM), lambda b, h, t: (0, h)),
                  pl.BlockSpec((1, HEAD_DIM), lambda b, h, t: (0, 0)),
                  pl.BlockSpec((SUB * HEAD_DIM, CHUNK), lambda b, h, t: (0, 0))],
        out_specs=pl.BlockSpec((tb, HEAD_DIM), lambda b, h, t: (b * nt + t, h)),
        out_shape=jax.ShapeDtypeStruct((B * T, BRANCH_WIDTH), BF16),
        scratch_shapes=[pltpu.VMEM((HEAD_DIM, HEAD_DIM), F32)],
        compiler_params=_cparams(("parallel", "parallel", "arbitrary")), name="hgrn2")(
            proj, proj, proj, proj, lb.reshape(1, BRANCH_WIDTH), norm_g.reshape(1, HEAD_DIM), _hgrn2_emat())


def _conv_silu(x_ref, p_ref, w_ref, first):
    x = x_ref[...]
    prev = jnp.where(first, 0.0, p_ref[...])
    xf = jnp.concatenate([prev, x], axis=0)
    tb = x.shape[0]
    w = w_ref[...]
    acc = x * w[GDN_CONV - 1:GDN_CONV]
    for j in range(GDN_CONV - 1):
        off = 8 - (GDN_CONV - 1) + j
        acc = acc + xf[off:off + tb] * w[j:j + 1]
    return _silu(acc)


def _l2n(x):
    return x * lax.rsqrt(jnp.sum(x * x, axis=-1, keepdims=True) + L2_EPS)


def _gdn_kernel(q_ref, k_ref, v_ref, pq_ref, pk_ref, pv_ref, z_ref, ab_ref, abt_ref,
                wq_ref, wk_ref, wv_ref, alog_ref, dt_ref, ng_ref, o_ref, s_ref, *, tb):
    h = pl.program_id(1)
    first = pl.program_id(2) == 0

    @pl.when(first)
    def _():
        s_ref[...] = jnp.zeros_like(s_ref)

    q_all = _l2n(_conv_silu(q_ref, pq_ref, wq_ref, first)) * (HEAD_DIM ** -0.5)
    k_all = _l2n(_conv_silu(k_ref, pk_ref, wk_ref, first))
    v_all = _conv_silu(v_ref, pv_ref, wv_ref, first)

    ab = ab_ref[...]
    lane = lax.broadcasted_iota(jnp.int32, ab.shape, 1)
    a_col = jnp.sum(jnp.where(lane == h, ab, 0.0), axis=1, keepdims=True)
    b_col = jnp.sum(jnp.where(lane == h + N_HEADS, ab, 0.0), axis=1, keepdims=True)
    abt = abt_ref[...]
    sub = lax.broadcasted_iota(jnp.int32, abt.shape, 0)
    a_row = jnp.sum(jnp.where(sub == h, abt, 0.0), axis=0, keepdims=True)
    neg_a = -jnp.exp(alog_ref[0])
    dt = dt_ref[0]
    g_col = neg_a * _softplus(a_col + dt)
    g_row = neg_a[:, :1] * _softplus(a_row + dt[:, :1])
    beta = _sigmoid(b_col)
    gc_all = _dot_exact_lhs(_block_tri(tb, True), g_col)
    gr_all = _dot_exact_rhs(jnp.broadcast_to(g_row, (8, tb)), _block_tri(tb, False))[0:1]

    ri = lax.broadcasted_iota(jnp.int32, (CHUNK, CHUNK), 0)
    ci = lax.broadcasted_iota(jnp.int32, (CHUNK, CHUNK), 1)
    causal = ri >= ci
    strict = ri > ci
    same_sub = (ri // SUB) == (ci // SUB)
    eye = jnp.where(ri == ci, 1.0, 0.0)
    neg_inf = -jnp.inf

    S = s_ref[...]
    outs = []
    for c in range(tb // CHUNK):
        sl = slice(c * CHUNK, (c + 1) * CHUNK)
        q, k, v, bt, gc = q_all[sl], k_all[sl], v_all[sl], beta[sl], gc_all[sl]
        gr = gr_all[:, c * CHUNK:(c + 1) * CHUNK]
        decay = jnp.exp(jnp.where(causal, gc[:, :CHUNK] - gr, neg_inf))
        kb = k * bt
        L = jnp.where(strict, _dot(kb, k, NT) * decay, 0.0)
        Ld = jnp.where(same_sub, L, 0.0)
        Lo = L - Ld
        X = eye - Ld
        P = _dot3(Ld, Ld)
        X = X + _dot3(X, P)
        P = _dot3(P, P)
        X = X + _dot3(X, P)
        P = _dot3(P, P)
        X = X + _dot3(X, P)
        M = _dot3(X, Lo)
        M2 = _dot3(M, M)
        Y = (eye - M) + _dot3(eye - M, M2)
        Tm = _dot3(Y, X)
        rhs = jnp.concatenate([v * bt, kb * jnp.exp(gc)], axis=1)
        uw = _dot3(Tm, rhs)
        u, w = uw[:, :HEAD_DIM], uw[:, HEAD_DIM:]
        qk = jnp.where(causal, _dot(q, k, NT) * decay, 0.0)
        v_new = u - _dot(w, S)
        o = _dot(q * jnp.exp(gc), S) + _dot(qk, v_new)
        g_last = gc[CHUNK - 1:CHUNK]
        S = S * jnp.exp(g_last)[:, :1] + _dot(k * jnp.exp(g_last - gc), v_new, TN)
        outs.append(o)
    s_ref[...] = S
    o = jnp.concatenate(outs, axis=0) if len(outs) > 1 else outs[0]
    o = o * lax.rsqrt(jnp.mean(o * o, axis=-1, keepdims=True) + RMS_EPS) * ng_ref[...]
    o_ref[...] = (o * _silu(z_ref[...])).astype(o_ref.dtype)


def _gdn(proj, ab_t, conv_w, a_log, dt_bias, norm_g, B, T):
    tb = _pick(T, (256, 128, 64))
    nt = T // tb
    r8 = tb // 8
    qb, kb_, vb, zb = (COL_GDN_QKV // LANE, COL_GDN_QKV // LANE + N_HEADS,
                       COL_GDN_QKV // LANE + 2 * N_HEADS, COL_GDN_Z // LANE)
    cb = lambda base: pl.BlockSpec((tb, HEAD_DIM), lambda b, h, t, base=base: (b * nt + t, base + h))
    pb = lambda base: pl.BlockSpec(
        (8, HEAD_DIM), lambda b, h, t, base=base: (jnp.maximum((b * nt + t) * r8 - 1, 0), base + h))
    wb = lambda base: pl.BlockSpec((GDN_CONV, HEAD_DIM), lambda b, h, t, base=base: (0, base + h))
    hv = pl.BlockSpec((1, 1, HEAD_DIM), lambda b, h, t: (h, 0, 0))
    bc = lambda p: jnp.broadcast_to(p.astype(F32)[:, None, None], (N_HEADS, 1, HEAD_DIM))
    return pl.pallas_call(
        functools.partial(_gdn_kernel, tb=tb), grid=(B, N_HEADS, nt),
        in_specs=[cb(qb), cb(kb_), cb(vb), pb(qb), pb(kb_), pb(vb), cb(zb),
                  pl.BlockSpec((tb, LANE), lambda b, h, t: (b * nt + t, COL_AB // LANE)),
                  pl.BlockSpec((2 * N_HEADS, tb), lambda b, h, t: (0, b * nt + t)),
                  wb(0), wb(N_HEADS), wb(2 * N_HEADS), hv, hv,
                  pl.BlockSpec((1, HEAD_DIM), lambda b, h, t: (0, 0))],
        out_specs=pl.BlockSpec((tb, HEAD_DIM), lambda b, h, t: (b * nt + t, h)),
        out_shape=jax.ShapeDtypeStruct((B * T, BRANCH_WIDTH), BF16),
        scratch_shapes=[pltpu.VMEM((HEAD_DIM, HEAD_DIM), F32)],
        compiler_params=_cparams(("parallel", "parallel", "arbitrary")), name="gated_deltanet")(
            proj, proj, proj, proj, proj, proj, proj, proj, ab_t,
            conv_w, conv_w, conv_w, bc(a_log), bc(dt_bias), norm_g.reshape(1, HEAD_DIM))


def _swa_kernel(q_ref, kc_ref, vc_ref, kp_ref, vp_ref, sink_ref, o_ref):
    has_prev = pl.program_id(1) > 0
    qi = lax.broadcasted_iota(jnp.int32, (SWA_BLOCK, SWA_BLOCK), 0)
    kj = lax.broadcasted_iota(jnp.int32, (SWA_BLOCK, SWA_BLOCK), 1)
    dist_c = (qi - kj).astype(F32)
    dist_p = dist_c + float(SWA_BLOCK)
    valid_c = qi >= kj
    valid_p = (kj > qi) & has_prev
    neg_inf = -jnp.inf
    group = SWA_Q_HEADS // SWA_KV_HEADS
    q_all = q_ref[...] * (SWA_HEAD_DIM ** -0.5)
    kc, vc, kp, vp = kc_ref[...], vc_ref[...], kp_ref[...], vp_ref[...]
    sinks = sink_ref[...]
    outs = []
    for hq in range(SWA_Q_HEADS):
        hk = hq // group
        slope = 2.0 ** (-8.0 * (hq + 1) / SWA_Q_HEADS)
        ks = slice(hk * SWA_HEAD_DIM, (hk + 1) * SWA_HEAD_DIM)
        q = q_all[:, hq * SWA_HEAD_DIM:(hq + 1) * SWA_HEAD_DIM]
        s_c = jnp.where(valid_c, _dot(q, kc[:, ks], NT) - slope * dist_c, neg_inf)
        s_p = jnp.where(valid_p, _dot(q, kp[:, ks], NT) - slope * dist_p, neg_inf)
        sink = sinks[hq:hq + 1, :1]
        m = jnp.maximum(jnp.maximum(jnp.max(s_c, axis=-1, keepdims=True),
                                    jnp.max(s_p, axis=-1, keepdims=True)), sink)
        p_c = jnp.exp(s_c - m)
        p_p = jnp.exp(s_p - m)
        den = (jnp.sum(p_c, axis=-1, keepdims=True) + jnp.sum(p_p, axis=-1, keepdims=True)
               + jnp.exp(sink - m))
        inv = 1.0 / den
        outs.append(_dot(p_c * inv, vc[:, ks]) + _dot(p_p * inv, vp[:, ks]))
    o_ref[...] = jnp.concatenate(outs, axis=1).astype(o_ref.dtype)


def _swa(proj, sinks, B, T):
    nb = T // SWA_BLOCK
    qw = SWA_Q_HEADS * SWA_HEAD_DIM
    cur = lambda col: pl.BlockSpec((SWA_BLOCK, LANE), lambda b, n, col=col: (b * nb + n, col))
    prv = lambda col: pl.BlockSpec((SWA_BLOCK, LANE),
                                   lambda b, n, col=col: (b * nb + jnp.maximum(n - 1, 0), col))
    sink_b = jnp.broadcast_to(sinks.astype(F32)[:, None], (SWA_Q_HEADS, LANE))
    return pl.pallas_call(
        _swa_kernel, grid=(B, nb),
        in_specs=[pl.BlockSpec((SWA_BLOCK, qw), lambda b, n: (b * nb + n, COL_SWA_Q // qw)),
                  cur(COL_SWA_K // LANE), cur(COL_SWA_V // LANE), prv(COL_SWA_K // LANE), prv(COL_SWA_V // LANE),
                  pl.BlockSpec((SWA_Q_HEADS, LANE), lambda b, n: (0, 0))],
        out_specs=pl.BlockSpec((SWA_BLOCK, qw), lambda b, n: (b * nb + n, 0)),
        out_shape=jax.ShapeDtypeStruct((B * T, qw), BF16),
        compiler_params=_cparams(("parallel", "arbitrary")), name="swa")(
            proj, proj, proj, proj, proj, sink_b)


def _prep_w_in(w):
    d = w.shape[0]
    main = w[:, :8192]
    ab = w[:, 8192:8208]
    swa = w[:, 8208:9488]
    gates = w[:, 9488:]
    pad = jnp.zeros((d, COL_GATES - COL_AB - 16), w.dtype)
    return jnp.concatenate([main, swa, ab, pad, gates], axis=1).astype(BF16)


def kernel(x, ln_in_g, ln_in_b, hg_lb_logits, w_in, gdn_conv_w, gdn_a_log, gdn_dt_bias, hg_norm_g,
           gdn_norm_g, swa_sinks, w_branch, w_out, ln1_g, ln1_b, w_gate_up, w_down, ln2_g, ln2_b):
    B, T, D = x.shape
    M = B * T
    depth = w_in.shape[0]
    lb_all = jnp.cumsum(jax.nn.softmax(hg_lb_logits.astype(F32), axis=0), axis=0)
    lb_all = lb_all - lb_all[0]

    h32, h16 = _layer_norm(x.reshape(M, D), ln_in_g, ln_in_b)
    for l in range(depth):
        proj = _matmul(h16, _prep_w_in(w_in[l]), F32, "in_proj")
        ab_t = proj[:, COL_AB:COL_AB + 2 * N_HEADS].T
        o_a = _hgrn2(proj, lb_all[l], hg_norm_g[l], B, T)
        o_b = _gdn(proj, ab_t, gdn_conv_w[l], gdn_a_log[l], gdn_dt_bias[l], gdn_norm_g[l], B, T)
        o_c = _swa(proj, swa_sinks[l], B, T)
        merged = _merge(o_a, o_b, o_c, w_branch[l].astype(BF16), proj, D)
        mix = _matmul(merged, w_out[l].astype(BF16), F32, "out_proj")
        h32, h16 = _layer_norm(mix, ln1_g[l], ln1_b[l], res=h32)
        ff = _glu_matmul(h16, w_gate_up[l].astype(BF16))
        ff = _matmul(ff, w_down[l].astype(BF16), F32, "ffn_down")
        h32, h16 = _layer_norm(ff, ln2_g[l], ln2_b[l], res=h32)
    return h32.reshape(B, T, D)
```

```python
import functools

import jax
import jax.numpy as jnp
import numpy as np
from jax import lax
from jax.experimental import pallas as pl
from jax.experimental.pallas import tpu as pltpu

F32 = jnp.float32
BF16 = jnp.bfloat16

N_HEADS = 8
HEAD_DIM = 128
BRANCH_WIDTH = N_HEADS * HEAD_DIM
GDN_CONV = 4
SWA_Q_HEADS = 16
SWA_KV_HEADS = 2
SWA_HEAD_DIM = 64
SWA_BLOCK = 128
CHUNK = 64
SUB = 16
DEPTH = 2
DEEPNORM_ALPHA = (2 * DEPTH) ** 0.25
LN_EPS = 1e-5
RMS_EPS = 1e-6
L2_EPS = 1e-6

COL_GDN_QKV = 0
COL_HG_Q, COL_HG_F, COL_HG_I, COL_HG_G = 3072, 4096, 5120, 6144
COL_GDN_Z = 7168
COL_SWA_Q, COL_SWA_K, COL_SWA_V = 8192, 9216, 9344
COL_AB = 9472
COL_GATES = 10240
LANE = 128
SUBLANE = 8

VMEM_LIMIT = 56 * 1024 * 1024


def _cparams(sem):
    return pltpu.CompilerParams(dimension_semantics=sem, vmem_limit_bytes=VMEM_LIMIT)


def _pick(n, cands):
    for c in cands:
        if n % c == 0:
            return c
    raise ValueError(f"no tile for {n} in {cands}")


def _sigmoid(x):
    return 1.0 / (1.0 + jnp.exp(-x))


def _silu(x):
    return x * _sigmoid(x)


def _log_sigmoid(x):
    return jnp.minimum(x, 0.0) - jnp.log1p(jnp.exp(-jnp.abs(x)))


def _softplus(x):
    return jnp.maximum(x, 0.0) + jnp.log1p(jnp.exp(-jnp.abs(x)))


def _logaddexp(a, b):
    return jnp.maximum(a, b) + jnp.log1p(jnp.exp(-jnp.abs(a - b)))


NN = (((1,), (0,)), ((), ()))
NT = (((1,), (1,)), ((), ()))
TN = (((0,), (0,)), ((), ()))


def _dot(a, b, dims=NN):
    return lax.dot_general(a.astype(BF16), b.astype(BF16), dims, preferred_element_type=F32)


def _split2(x):
    hi = x.astype(BF16)
    lo = (x - hi.astype(F32)).astype(BF16)
    return hi, lo


def _dot3s(a, b, dims=NN):
    dg = functools.partial(lax.dot_general, dimension_numbers=dims, preferred_element_type=F32)
    return dg(a[0], b[0]) + (dg(a[0], b[1]) + dg(a[1], b[0]))


def _dot3(a, b, dims=NN):
    return _dot3s(_split2(a), _split2(b), dims)


def _split3(x):
    hi = x.astype(BF16)
    r1 = x - hi.astype(F32)
    mid = r1.astype(BF16)
    lo = (r1 - mid.astype(F32)).astype(BF16)
    return hi, mid, lo


def _dot_exact_lhs(m_bf16, x):
    hi, mid, lo = _split3(x)
    dg = functools.partial(lax.dot_general, dimension_numbers=NN, preferred_element_type=F32)
    return dg(m_bf16, hi) + (dg(m_bf16, mid) + dg(m_bf16, lo))


def _dot_exact_rhs(x, m_bf16):
    hi, mid, lo = _split3(x)
    dg = functools.partial(lax.dot_general, dimension_numbers=NN, preferred_element_type=F32)
    return dg(hi, m_bf16) + (dg(mid, m_bf16) + dg(lo, m_bf16))


def _block_tri(n, lower):
    r = lax.broadcasted_iota(jnp.int32, (n, n), 0)
    c = lax.broadcasted_iota(jnp.int32, (n, n), 1)
    same = (r // CHUNK) == (c // CHUNK)
    tri = (r >= c) if lower else (r <= c)
    return jnp.where(same & tri, 1.0, 0.0).astype(BF16)


def _ln_core(x, g, b):
    mu = jnp.mean(x, axis=-1, keepdims=True)
    xc = x - mu
    var = jnp.mean(xc * xc, axis=-1, keepdims=True)
    return xc * lax.rsqrt(var + LN_EPS) * g + b


def _ln_kernel(x_ref, g_ref, b_ref, o32_ref, o16_ref):
    y = _ln_core(x_ref[...], g_ref[...], b_ref[...])
    o32_ref[...] = y
    o16_ref[...] = y.astype(BF16)


def _ln_res_kernel(h_ref, y_ref, g_ref, b_ref, o32_ref, o16_ref):
    y = _ln_core(DEEPNORM_ALPHA * h_ref[...] + y_ref[...], g_ref[...], b_ref[...])
    o32_ref[...] = y
    o16_ref[...] = y.astype(BF16)


def _layer_norm(x, g, b, res=None):
    M, D = x.shape
    tm = _pick(M, (256, 128, 64, 32, 16))
    row = pl.BlockSpec((tm, D), lambda i: (i, 0))
    vec = pl.BlockSpec((1, D), lambda i: (0, 0))
    g2, b2 = g.reshape(1, D), b.reshape(1, D)
    out_shape = (jax.ShapeDtypeStruct((M, D), F32), jax.ShapeDtypeStruct((M, D), BF16))
    if res is None:
        return pl.pallas_call(_ln_kernel, grid=(M // tm,), in_specs=[row, vec, vec],
                              out_specs=(row, row), out_shape=out_shape,
                              compiler_params=_cparams(("parallel",)), name="layer_norm")(x, g2, b2)
    return pl.pallas_call(_ln_res_kernel, grid=(M // tm,), in_specs=[row, row, vec, vec],
                          out_specs=(row, row), out_shape=out_shape,
                          compiler_params=_cparams(("parallel",)), name="layer_norm_res")(res, x, g2, b2)


def _mm_kernel(a_ref, w_ref, o_ref):
    o_ref[...] = jnp.dot(a_ref[...], w_ref[...], preferred_element_type=F32).astype(o_ref.dtype)


def _matmul(a, w, l, out_dtype, name):
    M, K = a.shape
    N = w.shape[2]
    tm = _pick(M, (1024, 512, 256, 128))
    tn = _pick(N, (1024, 512, 256, 128))
    if K > 8192:
        tm, tn = min(tm, 512), min(tn, 512)
    return pl.pallas_call(
        _mm_kernel, grid=(M // tm, N // tn),
        in_specs=[pl.BlockSpec((tm, K), lambda i, j: (i, 0)), pl.BlockSpec((None, K, tn), lambda i, j: (l, 0, j))],
        out_specs=pl.BlockSpec((tm, tn), lambda i, j: (i, j)),
        out_shape=jax.ShapeDtypeStruct((M, N), out_dtype),
        compiler_params=_cparams(("parallel", "parallel")), name=name)(a, w)


def _glu_kernel(a_ref, wg_ref, wu_ref, o_ref):
    a = a_ref[...]
    g = jnp.dot(a, wg_ref[...], preferred_element_type=F32)
    u = jnp.dot(a, wu_ref[...], preferred_element_type=F32)
    o_ref[...] = (_silu(g) * u).astype(o_ref.dtype)


def _glu_matmul(a, w_gate_up, l):
    M, K = a.shape
    F = w_gate_up.shape[2] // 2
    tm = _pick(M, (1024, 512, 256, 128))
    tn = _pick(F, (512, 256, 128))
    nf = F // tn
    return pl.pallas_call(
        _glu_kernel, grid=(M // tm, nf),
        in_specs=[pl.BlockSpec((tm, K), lambda i, j: (i, 0)),
                  pl.BlockSpec((None, K, tn), lambda i, j: (l, 0, j)),
                  pl.BlockSpec((None, K, tn), lambda i, j: (l, 0, j + nf))],
        out_specs=pl.BlockSpec((tm, tn), lambda i, j: (i, j)),
        out_shape=jax.ShapeDtypeStruct((M, F), BF16),
        compiler_params=_cparams(("parallel", "parallel")), name="ffn_gate_up")(a, w_gate_up, w_gate_up)


def _merge_kernel(oa_ref, ob_ref, oc_ref, wb_ref, g0_ref, g1_ref, g2_ref, o_ref):
    acc = _sigmoid(g0_ref[...]) * jnp.dot(oa_ref[...], wb_ref[0], preferred_element_type=F32)
    acc = acc + _sigmoid(g1_ref[...]) * jnp.dot(ob_ref[...], wb_ref[1], preferred_element_type=F32)
    acc = acc + _sigmoid(g2_ref[...]) * jnp.dot(oc_ref[...], wb_ref[2], preferred_element_type=F32)
    o_ref[...] = acc.astype(o_ref.dtype)


def _merge(o_a, o_b, o_c, w_branch, l, proj, d_model):
    M = o_a.shape[0]
    tm = _pick(M, (512, 256, 128))
    tn = _pick(d_model, (512, 256, 128))
    g_base = COL_GATES // tn
    g_step = d_model // tn
    o_spec = pl.BlockSpec((tm, BRANCH_WIDTH), lambda i, j: (i, 0))
    g_specs = [pl.BlockSpec((tm, tn), functools.partial(lambda i, j, b: (i, g_base + b * g_step + j), b=b))
               for b in range(3)]
    return pl.pallas_call(
        _merge_kernel, grid=(M // tm, d_model // tn),
        in_specs=[o_spec, o_spec, o_spec,
                  pl.BlockSpec((None, 3, BRANCH_WIDTH, tn), lambda i, j: (l, 0, 0, j))] + g_specs,
        out_specs=pl.BlockSpec((tm, tn), lambda i, j: (i, j)),
        out_shape=jax.ShapeDtypeStruct((M, d_model), BF16),
        compiler_params=_cparams(("parallel", "parallel")), name="branch_merge")(
            o_a, o_b, o_c, w_branch, proj, proj, proj)


def _hgrn2_kernel(q_ref, f_ref, i_ref, g_ref, lb_ref, ng_ref, e_ref, o_ref, st_ref, *, tb):
    @pl.when(pl.program_id(2) == 0)
    def _():
        st_ref[...] = jnp.zeros_like(st_ref)

    lb = lb_ref[...]
    log_lb = jnp.log(lb)
    log1m_lb = jnp.log1p(-lb)
    z = f_ref[...]
    log_f = _logaddexp(log_lb, log1m_lb + _log_sigmoid(z))
    k_all = (1.0 - lb) * _sigmoid(-z)
    q_all = _silu(q_ref[...])
    v_all = i_ref[...]
    g_all = _dot_exact_lhs(_block_tri(tb, True), log_f)

    row16 = lax.broadcasted_iota(jnp.int32, (SUB, HEAD_DIM), 0)
    row64 = lax.broadcasted_iota(jnp.int32, (CHUNK, HEAD_DIM), 0)
    r_blk = lax.broadcasted_iota(jnp.int32, (CHUNK, CHUNK), 0) // SUB
    c_blk = lax.broadcasted_iota(jnp.int32, (CHUNK, CHUNK), 1) // SUB
    e_mat = e_ref[...]
    n_sub = CHUNK // SUB
    neg_inf = -jnp.inf

    st = st_ref[...]
    outs = []
    for c in range(tb // CHUNK):
        sl = slice(c * CHUNK, (c + 1) * CHUNK)
        G, q, k, v = g_all[sl], q_all[sl], k_all[sl], v_all[sl]
        pcs = []
        for I in range(n_sub):
            s0 = I * SUB
            GI, qI = G[s0:s0 + SUB], q[s0:s0 + SUB]
            cols = []
            for j in range(SUB):
                r = s0 + j
                d = jnp.where(row16 >= j, GI - G[r:r + 1], neg_inf)
                cols.append((qI * jnp.exp(d) * k[r:r + 1]).astype(BF16))
            pcs.append(jnp.concatenate(cols, axis=1))
        pcat = jnp.concatenate(pcs, axis=0)
        dfull = jnp.dot(pcat, e_mat, preferred_element_type=F32)
        dfull = jnp.where(r_blk == c_blk, dfull, 0.0)
        a_rows = [dfull[0:SUB]]
        for I in range(1, n_sub):
            s0 = I * SUB
            gb = G[s0 - 1:s0]
            qt = q[s0:s0 + SUB] * jnp.exp(G[s0:s0 + SUB] - gb)
            kx = k * jnp.exp(jnp.where(row64 < s0, gb - G, neg_inf))
            a_rows.append(_dot(qt, kx, NT) + dfull[s0:s0 + SUB])
        a_mat = jnp.concatenate(a_rows, axis=0)
        o = _dot(a_mat, v) + _dot(q * jnp.exp(G), st, NT)
        g_last = G[CHUNK - 1:CHUNK]
        st = st * jnp.exp(g_last) + _dot(v, k * jnp.exp(g_last - G), TN)
        outs.append(o)
    st_ref[...] = st
    o = jnp.concatenate(outs, axis=0) if len(outs) > 1 else outs[0]
    o = o * lax.rsqrt(jnp.mean(o * o, axis=-1, keepdims=True) + RMS_EPS) * ng_ref[...]
    o_ref[...] = (o * _silu(g_ref[...])).astype(o_ref.dtype)


def _hgrn2_emat():
    e = np.zeros((SUB * HEAD_DIM, CHUNK), np.float32)
    for j in range(SUB):
        for I in range(CHUNK // SUB):
            e[j * HEAD_DIM:(j + 1) * HEAD_DIM, I * SUB + j] = 1.0
    return jnp.asarray(e, BF16)


def _hgrn2(proj, lb, norm_g, B, T):
    tb = _pick(T, (256, 128, 64))
    nt = T // tb
    cb = lambda base: pl.BlockSpec((tb, HEAD_DIM), lambda b, h, t, base=base: (b * nt + t, base + h))
    return pl.pallas_call(
        functools.partial(_hgrn2_kernel, tb=tb), grid=(B, N_HEADS, nt),
        in_specs=[cb(COL_HG_Q // LANE), cb(COL_HG_F // LANE), cb(COL_HG_I // LANE), cb(COL_HG_G // LANE),
                  pl.BlockSpec((1, HEAD_DIM), lambda b, h, t: (0, h)),
                  pl.BlockSpec((1, HEAD_DIM), lambda b, h, t: (0, 0)),
                  pl.BlockSpec((SUB * HEAD_DIM, CHUNK), lambda b, h, t: (0, 0))],
        out_specs=pl.BlockSpec((tb, HEAD_DIM), lambda b, h, t: (b * nt + t, h)),
        out_shape=jax.ShapeDtypeStruct((B * T, BRANCH_WIDTH), BF16),
        scratch_shapes=[pltpu.VMEM((HEAD_DIM, HEAD_DIM), F32)],
        compiler_params=_cparams(("parallel", "parallel", "arbitrary")), name="hgrn2")(
            proj, proj, proj, proj, lb.reshape(1, BRANCH_WIDTH), norm_g.reshape(1, HEAD_DIM), _hgrn2_emat())


def _l2n(x):
    return x * lax.rsqrt(jnp.sum(x * x, axis=-1, keepdims=True) + L2_EPS)


def _gdn_kernel(qkv_ref, prev_ref, z_ref, ab_ref, abt_ref, w_ref, hrow_ref, hcol_ref, ng_ref,
                o_ref, s_ref, *, tb):
    first = pl.program_id(1) == 0

    @pl.when(first)
    def _():
        s_ref[...] = jnp.zeros_like(s_ref)

    x = qkv_ref[...]
    prev = jnp.where(first, 0.0, prev_ref[...])
    xf = jnp.concatenate([prev, x], axis=0)
    w = w_ref[...]
    acc = x * w[GDN_CONV - 1:GDN_CONV]
    for j in range(GDN_CONV - 1):
        off = SUBLANE - (GDN_CONV - 1) + j
        acc = acc + xf[off:off + tb] * w[j:j + 1]
    y = _silu(acc)

    hrow = hrow_ref[...]
    hcol = hcol_ref[...]
    ab = ab_ref[...]
    g_cols = -jnp.exp(hrow[0:1]) * _softplus(ab + hrow[1:2])
    gc_cols = _dot_exact_lhs(_block_tri(tb, True), g_cols)
    beta_cols = _sigmoid(ab)
    abt = abt_ref[...]
    g_rows = -jnp.exp(hcol[:, 0:1]) * _softplus(abt[0:N_HEADS] + hcol[:, LANE:LANE + 1])
    gc_rows = _dot_exact_rhs(g_rows, _block_tri(tb, False))

    ri = lax.broadcasted_iota(jnp.int32, (CHUNK, CHUNK), 0)
    ci = lax.broadcasted_iota(jnp.int32, (CHUNK, CHUNK), 1)
    causal = ri >= ci
    strict = ri > ci
    same_sub = (ri // SUB) == (ci // SUB)
    eye = jnp.where(ri == ci, 1.0, 0.0)
    neg_inf = -jnp.inf
    n_chunk = tb // CHUNK
    pairs = [(h, c) for c in range(n_chunk) for h in range(N_HEADS)]

    def head_cols(base, h):
        return y[:, base + h * HEAD_DIM: base + (h + 1) * HEAD_DIM]

    q_h = [_l2n(head_cols(0, h)) * (HEAD_DIM ** -0.5) for h in range(N_HEADS)]
    k_h = [_l2n(head_cols(BRANCH_WIDTH, h)) for h in range(N_HEADS)]
    v_h = [head_cols(2 * BRANCH_WIDTH, h) for h in range(N_HEADS)]

    rows = lambda c: slice(c * CHUNK, (c + 1) * CHUNK)
    q = {(h, c): q_h[h][rows(c)] for h, c in pairs}
    k = {(h, c): k_h[h][rows(c)] for h, c in pairs}
    v = {(h, c): v_h[h][rows(c)] for h, c in pairs}
    gc = {(h, c): gc_cols[rows(c), h:h + 1] for h, c in pairs}
    bt = {(h, c): beta_cols[rows(c), N_HEADS + h:N_HEADS + h + 1] for h, c in pairs}
    decay = {(h, c): jnp.exp(jnp.where(causal, gc[h, c] - gc_rows[h:h + 1, rows(c)], neg_inf)) for h, c in pairs}
    kb = {p: k[p] * bt[p] for p in pairs}
    L = {p: jnp.where(strict, _dot(kb[p], k[p], NT) * decay[p], 0.0) for p in pairs}
    qk = {p: jnp.where(causal, _dot(q[p], k[p], NT) * decay[p], 0.0) for p in pairs}
    Ld = {p: jnp.where(same_sub, L[p], 0.0) for p in pairs}
    Lo = {p: L[p] - Ld[p] for p in pairs}
    X = {p: eye - Ld[p] for p in pairs}
    Ls = {p: _split2(Ld[p]) for p in pairs}
    P = {p: _dot3s(Ls[p], Ls[p]) for p in pairs}
    for it in range(3):
        Ps = {p: _split2(P[p]) for p in pairs}
        X = {p: X[p] + _dot3s(_split2(X[p]), Ps[p]) for p in pairs}
        if it < 2:
            P = {p: _dot3s(Ps[p], Ps[p]) for p in pairs}
    Xs = {p: _split2(X[p]) for p in pairs}
    M = {p: _dot3s(Xs[p], _split2(Lo[p])) for p in pairs}
    Ms = {p: _split2(M[p]) for p in pairs}
    M2 = {p: _dot3s(Ms[p], Ms[p]) for p in pairs}
    Y = {p: (eye - M[p]) + _dot3(eye - M[p], M2[p]) for p in pairs}
    Tm = {p: _dot3s(_split2(Y[p]), Xs[p]) for p in pairs}
    uw = {p: _dot3(Tm[p], jnp.concatenate([v[p] * bt[p], kb[p] * jnp.exp(gc[p])], axis=1)) for p in pairs}
    qe = {p: q[p] * jnp.exp(gc[p]) for p in pairs}
    g_last = {p: gc[p][CHUNK - 1:CHUNK] for p in pairs}
    kd = {p: k[p] * jnp.exp(g_last[p] - gc[p]) for p in pairs}

    S = [s_ref[h] for h in range(N_HEADS)]
    o_rows = []
    for c in range(n_chunk):
        v_new = [uw[h, c][:, :HEAD_DIM] - _dot(uw[h, c][:, HEAD_DIM:], S[h]) for h in range(N_HEADS)]
        o_c = [_dot(qe[h, c], S[h]) + _dot(qk[h, c], v_new[h]) for h in range(N_HEADS)]
        S = [S[h] * jnp.exp(g_last[h, c]) + _dot(kd[h, c], v_new[h], TN) for h in range(N_HEADS)]
        o_rows.append(o_c)
    for h in range(N_HEADS):
        s_ref[h] = S[h]
    ng = ng_ref[...]
    z = z_ref[...]
    outs = []
    for h in range(N_HEADS):
        o = jnp.concatenate([o_rows[c][h] for c in range(n_chunk)], axis=0) if n_chunk > 1 else o_rows[0][h]
        o = o * lax.rsqrt(jnp.mean(o * o, axis=-1, keepdims=True) + RMS_EPS) * ng
        outs.append((o * _silu(z[:, h * HEAD_DIM:(h + 1) * HEAD_DIM])).astype(o_ref.dtype))
    o_ref[...] = jnp.concatenate(outs, axis=1)


def _gdn(proj, ab_t, conv_w, a_log, dt_bias, norm_g, B, T):
    tb = _pick(T, (128, 64))
    nt = T // tb
    r8 = tb // SUBLANE
    w3 = 3 * BRANCH_WIDTH
    pad = lambda p: jnp.pad(p.astype(F32), (0, LANE - N_HEADS))
    hrow = jnp.stack([pad(a_log), pad(dt_bias)])
    bc = lambda p: jnp.broadcast_to(p.astype(F32)[:, None], (N_HEADS, LANE))
    hcol = jnp.concatenate([bc(a_log), bc(dt_bias)], axis=1)
    const = lambda shape: pl.BlockSpec(shape, lambda b, t: (0,) * len(shape))
    return pl.pallas_call(
        functools.partial(_gdn_kernel, tb=tb), grid=(B, nt),
        in_specs=[pl.BlockSpec((tb, w3), lambda b, t: (b * nt + t, COL_GDN_QKV // w3)),
                  pl.BlockSpec((SUBLANE, w3),
                               lambda b, t: (jnp.maximum((b * nt + t) * r8 - 1, 0), COL_GDN_QKV // w3)),
                  pl.BlockSpec((tb, BRANCH_WIDTH), lambda b, t: (b * nt + t, COL_GDN_Z // BRANCH_WIDTH)),
                  pl.BlockSpec((tb, LANE), lambda b, t: (b * nt + t, COL_AB // LANE)),
                  pl.BlockSpec((2 * N_HEADS, tb), lambda b, t: (0, b * nt + t)),
                  const((GDN_CONV, w3)), const((2, LANE)), const((N_HEADS, 2 * LANE)), const((1, HEAD_DIM))],
        out_specs=pl.BlockSpec((tb, BRANCH_WIDTH), lambda b, t: (b * nt + t, 0)),
        out_shape=jax.ShapeDtypeStruct((B * T, BRANCH_WIDTH), BF16),
        scratch_shapes=[pltpu.VMEM((N_HEADS, HEAD_DIM, HEAD_DIM), F32)],
        compiler_params=_cparams(("parallel", "arbitrary")), name="gated_deltanet")(
            proj, proj, proj, proj, ab_t, conv_w, hrow, hcol, norm_g.reshape(1, HEAD_DIM))


def _swa_kernel(q_ref, kc_ref, vc_ref, kp_ref, vp_ref, sink_ref, o_ref):
    has_prev = pl.program_id(1) > 0
    qi = lax.broadcasted_iota(jnp.int32, (SWA_BLOCK, SWA_BLOCK), 0)
    kj = lax.broadcasted_iota(jnp.int32, (SWA_BLOCK, SWA_BLOCK), 1)
    dist_c = (qi - kj).astype(F32)
    dist_p = dist_c + float(SWA_BLOCK)
    valid_c = qi >= kj
    valid_p = (kj > qi) & has_prev
    neg_inf = -jnp.inf
    group = SWA_Q_HEADS // SWA_KV_HEADS
    q_all = q_ref[...] * (SWA_HEAD_DIM ** -0.5)
    kc, vc, kp, vp = kc_ref[...], vc_ref[...], kp_ref[...], vp_ref[...]
    sinks = sink_ref[...]
    heads = range(SWA_Q_HEADS)
    ks = [slice((hq // group) * SWA_HEAD_DIM, (hq // group + 1) * SWA_HEAD_DIM) for hq in heads]
    slope = [2.0 ** (-8.0 * (hq + 1) / SWA_Q_HEADS) for hq in heads]
    q = [q_all[:, hq * SWA_HEAD_DIM:(hq + 1) * SWA_HEAD_DIM].astype(BF16) for hq in heads]
    kcb, kpb, vcb, vpb = kc.astype(BF16), kp.astype(BF16), vc.astype(BF16), vp.astype(BF16)
    s_c = [jnp.where(valid_c, _dot(q[h], kcb[:, ks[h]], NT) - slope[h] * dist_c, neg_inf) for h in heads]
    s_p = [jnp.where(valid_p, _dot(q[h], kpb[:, ks[h]], NT) - slope[h] * dist_p, neg_inf) for h in heads]
    sink = [sinks[h:h + 1, :1] for h in heads]
    m = [jnp.maximum(jnp.max(jnp.maximum(s_c[h], s_p[h]), axis=-1, keepdims=True), sink[h]) for h in heads]
    p_c = [jnp.exp(s_c[h] - m[h]) for h in heads]
    p_p = [jnp.exp(s_p[h] - m[h]) for h in heads]
    inv = [1.0 / (jnp.sum(p_c[h] + p_p[h], axis=-1, keepdims=True) + jnp.exp(sink[h] - m[h])) for h in heads]
    outs = [_dot(p_c[h] * inv[h], vcb[:, ks[h]]) + _dot(p_p[h] * inv[h], vpb[:, ks[h]]) for h in heads]
    o_ref[...] = jnp.concatenate(outs, axis=1).astype(o_ref.dtype)


def _swa(proj, sinks, B, T):
    nb = T // SWA_BLOCK
    qw = SWA_Q_HEADS * SWA_HEAD_DIM
    cur = lambda col: pl.BlockSpec((SWA_BLOCK, LANE), lambda b, n, col=col: (b * nb + n, col))
    prv = lambda col: pl.BlockSpec((SWA_BLOCK, LANE),
                                   lambda b, n, col=col: (b * nb + jnp.maximum(n - 1, 0), col))
    sink_b = jnp.broadcast_to(sinks.astype(F32)[:, None], (SWA_Q_HEADS, LANE))
    return pl.pallas_call(
        _swa_kernel, grid=(B, nb),
        in_specs=[pl.BlockSpec((SWA_BLOCK, qw), lambda b, n: (b * nb + n, COL_SWA_Q // qw)),
                  cur(COL_SWA_K // LANE), cur(COL_SWA_V // LANE), prv(COL_SWA_K // LANE), prv(COL_SWA_V // LANE),
                  pl.BlockSpec((SWA_Q_HEADS, LANE), lambda b, n: (0, 0))],
        out_specs=pl.BlockSpec((SWA_BLOCK, qw), lambda b, n: (b * nb + n, 0)),
        out_shape=jax.ShapeDtypeStruct((B * T, qw), BF16),
        compiler_params=_cparams(("parallel", "arbitrary")), name="swa")(
            proj, proj, proj, proj, proj, sink_b)


SRC_HG, SRC_QKV, SRC_Z, SRC_AB = 0, 4096, 7168, 8192
N_AB = 2 * N_HEADS
SWA_COLS = COL_AB - COL_SWA_Q


def _w_in_kernel(w_ref, o_ref):
    o_ref[:, COL_GDN_QKV:COL_GDN_QKV + 3 * BRANCH_WIDTH] = (
        w_ref[:, SRC_QKV:SRC_QKV + 3 * BRANCH_WIDTH].astype(BF16))
    o_ref[:, COL_HG_Q:COL_HG_Q + 4 * BRANCH_WIDTH] = w_ref[:, SRC_HG:SRC_HG + 4 * BRANCH_WIDTH].astype(BF16)
    o_ref[:, COL_GDN_Z:COL_GDN_Z + BRANCH_WIDTH] = w_ref[:, SRC_Z:SRC_Z + BRANCH_WIDTH].astype(BF16)
    tail = w_ref[:, SRC_AB:]
    rest = tail[:, N_AB:].astype(BF16)
    o_ref[:, COL_SWA_Q:COL_AB] = rest[:, :SWA_COLS]
    o_ref[:, COL_AB:COL_GATES] = jnp.concatenate(
        [tail[:, :N_AB].astype(BF16), jnp.zeros((tail.shape[0], COL_GATES - COL_AB - N_AB), BF16)], axis=1)
    o_ref[:, COL_GATES:] = rest[:, SWA_COLS:]


def _prep_w_in(w):
    L, d, n_src = w.shape
    n_out = COL_GATES + (n_src - SRC_AB - N_AB - SWA_COLS)
    tr = _pick(d, (64, 32, 16))
    return pl.pallas_call(
        _w_in_kernel, grid=(L, d // tr),
        in_specs=[pl.BlockSpec((None, tr, n_src), lambda l, i: (l, i, 0))],
        out_specs=pl.BlockSpec((None, tr, n_out), lambda l, i: (l, i, 0)),
        out_shape=jax.ShapeDtypeStruct((L, d, n_out), BF16),
        compiler_params=_cparams(("parallel", "parallel")), name="w_in_relayout")(w)


def kernel(x, ln_in_g, ln_in_b, hg_lb_logits, w_in, gdn_conv_w, gdn_a_log, gdn_dt_bias, hg_norm_g,
           gdn_norm_g, swa_sinks, w_branch, w_out, ln1_g, ln1_b, w_gate_up, w_down, ln2_g, ln2_b):
    B, T, D = x.shape
    M = B * T
    depth = w_in.shape[0]
    lb_all = jnp.cumsum(jax.nn.softmax(hg_lb_logits.astype(F32), axis=0), axis=0)
    lb_all = lb_all - lb_all[0]

    w_in16 = _prep_w_in(w_in)
    w_branch16, w_out16 = w_branch.astype(BF16), w_out.astype(BF16)
    w_gate_up16, w_down16 = w_gate_up.astype(BF16), w_down.astype(BF16)

    h32, h16 = _layer_norm(x.reshape(M, D), ln_in_g, ln_in_b)
    for l in range(depth):
        proj = _matmul(h16, w_in16, l, F32, "in_proj")
        ab_t = proj[:, COL_AB:COL_AB + N_AB].T
        o_a = _hgrn2(proj, lb_all[l], hg_norm_g[l], B, T)
        o_b = _gdn(proj, ab_t, gdn_conv_w[l], gdn_a_log[l], gdn_dt_bias[l], gdn_norm_g[l], B, T)
        o_c = _swa(proj, swa_sinks[l], B, T)
        merged = _merge(o_a, o_b, o_c, w_branch16, l, proj, D)
        mix = _matmul(merged, w_out16, l, F32, "out_proj")
        h32, h16 = _layer_norm(mix, ln1_g[l], ln1_b[l], res=h32)
        ff = _glu_matmul(h16, w_gate_up16, l)
        ff = _matmul(ff, w_down16, l, F32, "ffn_down")
        h32, h16 = _layer_norm(ff, ln2_g[l], ln2_b[l], res=h32)
    return h32.reshape(B, T, D)
```

```python
import functools
import math

import jax
import jax.numpy as jnp
import numpy as np
from jax import lax
from jax.experimental import pallas as pl
from jax.experimental.pallas import tpu as pltpu

F32 = jnp.float32
BF16 = jnp.bfloat16

N_HEADS = 8
HEAD_DIM = 128
BRANCH_WIDTH = N_HEADS * HEAD_DIM
GDN_CONV = 4
SWA_Q_HEADS = 16
SWA_KV_HEADS = 2
SWA_HEAD_DIM = 64
SWA_BLOCK = 128
CHUNK = 64
SUB = 16
DEPTH = 2
DEEPNORM_ALPHA = (2 * DEPTH) ** 0.25
LN_EPS = 1e-5
RMS_EPS = 1e-6
L2_EPS = 1e-6

COL_GDN_QKV = 0
COL_HG_Q, COL_HG_F, COL_HG_I, COL_HG_G = 3072, 4096, 5120, 6144
COL_GDN_Z = 7168
COL_SWA_Q, COL_SWA_K, COL_SWA_V = 8192, 9216, 9344
COL_AB = 9472
COL_GATES = 10240
LANE = 128
SUBLANE = 8

VMEM_LIMIT = 56 * 1024 * 1024


def _cparams(sem):
    return pltpu.CompilerParams(dimension_semantics=sem, vmem_limit_bytes=VMEM_LIMIT)


def _pick(n, cands):
    for c in cands:
        if n % c == 0:
            return c
    raise ValueError(f"no tile for {n} in {cands}")


def _sigmoid(x):
    return 1.0 / (1.0 + jnp.exp(-x))


def _silu(x):
    return x * _sigmoid(x)


def _log_sigmoid(x):
    return jnp.minimum(x, 0.0) - jnp.log1p(jnp.exp(-jnp.abs(x)))


def _softplus(x):
    return jnp.maximum(x, 0.0) + jnp.log1p(jnp.exp(-jnp.abs(x)))


def _logaddexp(a, b):
    return jnp.maximum(a, b) + jnp.log1p(jnp.exp(-jnp.abs(a - b)))


NN = (((1,), (0,)), ((), ()))
NT = (((1,), (1,)), ((), ()))
TN = (((0,), (0,)), ((), ()))


def _dot(a, b, dims=NN):
    return lax.dot_general(a.astype(BF16), b.astype(BF16), dims, preferred_element_type=F32)


def _split2(x):
    hi = x.astype(BF16)
    lo = (x - hi.astype(F32)).astype(BF16)
    return hi, lo


def _dot3s(a, b, dims=NN):
    dg = functools.partial(lax.dot_general, dimension_numbers=dims, preferred_element_type=F32)
    return dg(a[0], b[0]) + (dg(a[0], b[1]) + dg(a[1], b[0]))


def _dot3(a, b, dims=NN):
    return _dot3s(_split2(a), _split2(b), dims)


def _dotp(a, b, passes):
    return _dot3(a, b) if passes == 3 else _dot(a, b)


GDN_PASSES = {"L2": 3, "X2": 3, "L4": 1, "X4": 1, "L8": 1, "X8": 1, "M": 3, "M2": 1, "Y": 1, "T": 1, "UW": 1}


def _split3(x):
    hi = x.astype(BF16)
    r1 = x - hi.astype(F32)
    mid = r1.astype(BF16)
    lo = (r1 - mid.astype(F32)).astype(BF16)
    return hi, mid, lo


def _dot_exact_lhs(m_bf16, x):
    hi, mid, lo = _split3(x)
    dg = functools.partial(lax.dot_general, dimension_numbers=NN, preferred_element_type=F32)
    return dg(m_bf16, hi) + (dg(m_bf16, mid) + dg(m_bf16, lo))


def _dot_exact_rhs(x, m_bf16):
    hi, mid, lo = _split3(x)
    dg = functools.partial(lax.dot_general, dimension_numbers=NN, preferred_element_type=F32)
    return dg(hi, m_bf16) + (dg(mid, m_bf16) + dg(lo, m_bf16))


def _block_tri(n, lower):
    r = lax.broadcasted_iota(jnp.int32, (n, n), 0)
    c = lax.broadcasted_iota(jnp.int32, (n, n), 1)
    same = (r // CHUNK) == (c // CHUNK)
    tri = (r >= c) if lower else (r <= c)
    return jnp.where(same & tri, 1.0, 0.0).astype(BF16)


def _ln_core(x, g, b):
    mu = jnp.mean(x, axis=-1, keepdims=True)
    xc = x - mu
    var = jnp.mean(xc * xc, axis=-1, keepdims=True)
    return xc * lax.rsqrt(var + LN_EPS) * g + b


def _ln_kernel(x_ref, g_ref, b_ref, o32_ref, o16_ref):
    y = _ln_core(x_ref[...], g_ref[...], b_ref[...])
    o32_ref[...] = y
    o16_ref[...] = y.astype(BF16)


def _ln_res_kernel(h_ref, y_ref, g_ref, b_ref, o32_ref, o16_ref):
    y = _ln_core(DEEPNORM_ALPHA * h_ref[...] + y_ref[...], g_ref[...], b_ref[...])
    o32_ref[...] = y
    o16_ref[...] = y.astype(BF16)


def _layer_norm(x, g, b, res=None):
    M, D = x.shape
    tm = _pick(M, (256, 128, 64, 32, 16))
    row = pl.BlockSpec((tm, D), lambda i: (i, 0))
    vec = pl.BlockSpec((1, D), lambda i: (0, 0))
    g2, b2 = g.reshape(1, D), b.reshape(1, D)
    out_shape = (jax.ShapeDtypeStruct((M, D), F32), jax.ShapeDtypeStruct((M, D), BF16))
    if res is None:
        return pl.pallas_call(_ln_kernel, grid=(M // tm,), in_specs=[row, vec, vec],
                              out_specs=(row, row), out_shape=out_shape,
                              compiler_params=_cparams(("parallel",)), name="layer_norm")(x, g2, b2)
    return pl.pallas_call(_ln_res_kernel, grid=(M // tm,), in_specs=[row, row, vec, vec],
                          out_specs=(row, row), out_shape=out_shape,
                          compiler_params=_cparams(("parallel",)), name="layer_norm_res")(res, x, g2, b2)


def _mm_kernel(a_ref, w_ref, o_ref):
    o_ref[...] = jnp.dot(a_ref[...], w_ref[...], preferred_element_type=F32).astype(o_ref.dtype)


def _mm_nt_kernel(a_ref, w_ref, o_ref):
    o_ref[...] = lax.dot_general(a_ref[...], w_ref[...], NT, preferred_element_type=F32).astype(o_ref.dtype)


def _matmul(a, w, l, out_dtype, name, w_is_nk=False, cols=None):
    M, K = a.shape
    n_all = w.shape[1] if w_is_nk else w.shape[2]
    c0, c1 = cols if cols is not None else (0, n_all)
    N = c1 - c0
    tm = _pick(M, (1024, 512, 256, 128))
    tn = _pick(math.gcd(N, c0) if c0 else N, (1024, 512, 256, 128))
    if K > 8192:
        tm, tn = min(tm, 512), min(tn, 512)
    j0 = c0 // tn
    w_spec = (pl.BlockSpec((None, tn, K), lambda i, j: (l, j0 + j, 0)) if w_is_nk
              else pl.BlockSpec((None, K, tn), lambda i, j: (l, 0, j0 + j)))
    return pl.pallas_call(
        _mm_nt_kernel if w_is_nk else _mm_kernel, grid=(M // tm, N // tn),
        in_specs=[pl.BlockSpec((tm, K), lambda i, j: (i, 0)), w_spec],
        out_specs=pl.BlockSpec((tm, tn), lambda i, j: (i, j)),
        out_shape=jax.ShapeDtypeStruct((M, N), out_dtype),
        compiler_params=_cparams(("parallel", "parallel")), name=name)(a, w)


def _glu_kernel(a_ref, wg_ref, wu_ref, o_ref):
    a = a_ref[...]
    g = jnp.dot(a, wg_ref[...], preferred_element_type=F32)
    u = jnp.dot(a, wu_ref[...], preferred_element_type=F32)
    o_ref[...] = (_silu(g) * u).astype(o_ref.dtype)


def _glu_matmul(a, w_gate_up, l):
    M, K = a.shape
    F = w_gate_up.shape[2] // 2
    tm = _pick(M, (1024, 512, 256, 128))
    tn = _pick(F, (512, 256, 128))
    nf = F // tn
    return pl.pallas_call(
        _glu_kernel, grid=(M // tm, nf),
        in_specs=[pl.BlockSpec((tm, K), lambda i, j: (i, 0)),
                  pl.BlockSpec((None, K, tn), lambda i, j: (l, 0, j)),
                  pl.BlockSpec((None, K, tn), lambda i, j: (l, 0, j + nf))],
        out_specs=pl.BlockSpec((tm, tn), lambda i, j: (i, j)),
        out_shape=jax.ShapeDtypeStruct((M, F), BF16),
        compiler_params=_cparams(("parallel", "parallel")), name="ffn_gate_up")(a, w_gate_up, w_gate_up)


def _merge_kernel(oa_ref, ob_ref, oc_ref, wb_ref, g0_ref, g1_ref, g2_ref, o_ref):
    gate = lambda g_ref: _sigmoid(g_ref[...].astype(F32))
    acc = gate(g0_ref) * jnp.dot(oa_ref[...], wb_ref[0], preferred_element_type=F32)
    acc = acc + gate(g1_ref) * jnp.dot(ob_ref[...], wb_ref[1], preferred_element_type=F32)
    acc = acc + gate(g2_ref) * jnp.dot(oc_ref[...], wb_ref[2], preferred_element_type=F32)
    o_ref[...] = acc.astype(o_ref.dtype)


def _merge(o_a, o_b, o_c, w_branch, l, gate_logits, d_model):
    M = o_a.shape[0]
    tm = _pick(M, (1024, 512, 256, 128))
    tn = _pick(d_model, (512, 256, 128))
    g_step = d_model // tn
    o_spec = pl.BlockSpec((tm, BRANCH_WIDTH), lambda i, j: (i, 0))
    g_specs = [pl.BlockSpec((tm, tn), functools.partial(lambda i, j, b: (i, b * g_step + j), b=b))
               for b in range(3)]
    return pl.pallas_call(
        _merge_kernel, grid=(M // tm, d_model // tn),
        in_specs=[o_spec, o_spec, o_spec,
                  pl.BlockSpec((None, 3, BRANCH_WIDTH, tn), lambda i, j: (l, 0, 0, j))] + g_specs,
        out_specs=pl.BlockSpec((tm, tn), lambda i, j: (i, j)),
        out_shape=jax.ShapeDtypeStruct((M, d_model), BF16),
        compiler_params=_cparams(("parallel", "parallel")), name="branch_merge")(
            o_a, o_b, o_c, w_branch, gate_logits, gate_logits, gate_logits)


def _hgrn2_kernel(q_ref, f_ref, i_ref, g_ref, lb_ref, ng_ref, e_ref, o_ref, st_ref, *, tb):
    @pl.when(pl.program_id(2) == 0)
    def _():
        st_ref[...] = jnp.zeros_like(st_ref)

    lb = lb_ref[...]
    log_lb = jnp.log(lb)
    log1m_lb = jnp.log1p(-lb)
    z = f_ref[...]
    log_f = _logaddexp(log_lb, log1m_lb + _log_sigmoid(z))
    k_all = (1.0 - lb) * _sigmoid(-z)
    q_all = _silu(q_ref[...])
    v_all = i_ref[...]
    g_all = _dot_exact_lhs(_block_tri(tb, True), log_f)

    row16 = lax.broadcasted_iota(jnp.int32, (SUB, HEAD_DIM), 0)
    row64 = lax.broadcasted_iota(jnp.int32, (CHUNK, HEAD_DIM), 0)
    r_blk = lax.broadcasted_iota(jnp.int32, (CHUNK, CHUNK), 0) // SUB
    c_blk = lax.broadcasted_iota(jnp.int32, (CHUNK, CHUNK), 1) // SUB
    e_mat = e_ref[...]
    n_sub = CHUNK // SUB
    neg_inf = -jnp.inf

    st = st_ref[...]
    outs = []
    for c in range(tb // CHUNK):
        sl = slice(c * CHUNK, (c + 1) * CHUNK)
        G, q, k, v = g_all[sl], q_all[sl], k_all[sl], v_all[sl]
        pcs = []
        for I in range(n_sub):
            s0 = I * SUB
            GI, qI = G[s0:s0 + SUB], q[s0:s0 + SUB]
            cols = []
            for j in range(SUB):
                r = s0 + j
                d = jnp.where(row16 >= j, GI - G[r:r + 1], neg_inf)
                cols.append((qI * jnp.exp(d) * k[r:r + 1]).astype(BF16))
            pcs.append(jnp.concatenate(cols, axis=1))
        pcat = jnp.concatenate(pcs, axis=0)
        dfull = jnp.dot(pcat, e_mat, preferred_element_type=F32)
        dfull = jnp.where(r_blk == c_blk, dfull, 0.0)
        a_rows = [dfull[0:SUB]]
        for I in range(1, n_sub):
            s0 = I * SUB
            gb = G[s0 - 1:s0]
            qt = q[s0:s0 + SUB] * jnp.exp(G[s0:s0 + SUB] - gb)
            kx = k * jnp.exp(jnp.where(row64 < s0, gb - G, neg_inf))
            a_rows.append(_dot(qt, kx, NT) + dfull[s0:s0 + SUB])
        a_mat = jnp.concatenate(a_rows, axis=0)
        o = _dot(a_mat, v) + _dot(q * jnp.exp(G), st, NT)
        g_last = G[CHUNK - 1:CHUNK]
        st = st * jnp.exp(g_last) + _dot(v, k * jnp.exp(g_last - G), TN)
        outs.append(o)
    st_ref[...] = st
    o = jnp.concatenate(outs, axis=0) if len(outs) > 1 else outs[0]
    o = o * lax.rsqrt(jnp.mean(o * o, axis=-1, keepdims=True) + RMS_EPS) * ng_ref[...]
    o_ref[...] = (o * _silu(g_ref[...])).astype(o_ref.dtype)


def _hgrn2_emat():
    e = np.zeros((SUB * HEAD_DIM, CHUNK), np.float32)
    for j in range(SUB):
        for I in range(CHUNK // SUB):
            e[j * HEAD_DIM:(j + 1) * HEAD_DIM, I * SUB + j] = 1.0
    return jnp.asarray(e, BF16)


def _hgrn2(proj, lb, norm_g, B, T):
    tb = _pick(T, (256, 128, 64))
    nt = T // tb
    cb = lambda base: pl.BlockSpec((tb, HEAD_DIM), lambda b, h, t, base=base: (b * nt + t, base + h))
    return pl.pallas_call(
        functools.partial(_hgrn2_kernel, tb=tb), grid=(B, N_HEADS, nt),
        in_specs=[cb(COL_HG_Q // LANE), cb(COL_HG_F // LANE), cb(COL_HG_I // LANE), cb(COL_HG_G // LANE),
                  pl.BlockSpec((1, HEAD_DIM), lambda b, h, t: (0, h)),
                  pl.BlockSpec((1, HEAD_DIM), lambda b, h, t: (0, 0)),
                  pl.BlockSpec((SUB * HEAD_DIM, CHUNK), lambda b, h, t: (0, 0))],
        out_specs=pl.BlockSpec((tb, HEAD_DIM), lambda b, h, t: (b * nt + t, h)),
        out_shape=jax.ShapeDtypeStruct((B * T, BRANCH_WIDTH), BF16),
        scratch_shapes=[pltpu.VMEM((HEAD_DIM, HEAD_DIM), F32)],
        compiler_params=_cparams(("parallel", "parallel", "arbitrary")), name="hgrn2")(
            proj, proj, proj, proj, lb.reshape(1, BRANCH_WIDTH), norm_g.reshape(1, HEAD_DIM), _hgrn2_emat())


def _l2n(x):
    return x * lax.rsqrt(jnp.sum(x * x, axis=-1, keepdims=True) + L2_EPS)


def _gdn_kernel(qkv_ref, prev_ref, z_ref, ab_ref, abt_ref, w_ref, hrow_ref, hcol_ref, ng_ref,
                o_ref, s_ref, *, tb):
    first = pl.program_id(1) == 0

    @pl.when(first)
    def _():
        s_ref[...] = jnp.zeros_like(s_ref)

    x = qkv_ref[...]
    prev = jnp.where(first, 0.0, prev_ref[...])
    xf = jnp.concatenate([prev, x], axis=0)
    w = w_ref[...]
    acc = x * w[GDN_CONV - 1:GDN_CONV]
    for j in range(GDN_CONV - 1):
        off = SUBLANE - (GDN_CONV - 1) + j
        acc = acc + xf[off:off + tb] * w[j:j + 1]
    y = _silu(acc)

    hrow = hrow_ref[...]
    hcol = hcol_ref[...]
    ab = ab_ref[...]
    g_cols = -jnp.exp(hrow[0:1]) * _softplus(ab + hrow[1:2])
    gc_cols = _dot_exact_lhs(_block_tri(tb, True), g_cols)
    beta_cols = _sigmoid(ab)
    abt = abt_ref[...]
    g_rows = -jnp.exp(hcol[:, 0:1]) * _softplus(abt[0:N_HEADS] + hcol[:, LANE:LANE + 1])
    gc_rows = _dot_exact_rhs(g_rows, _block_tri(tb, False))

    ri = lax.broadcasted_iota(jnp.int32, (CHUNK, CHUNK), 0)
    ci = lax.broadcasted_iota(jnp.int32, (CHUNK, CHUNK), 1)
    causal = ri >= ci
    strict = ri > ci
    same_sub = (ri // SUB) == (ci // SUB)
    eye = jnp.where(ri == ci, 1.0, 0.0)
    neg_inf = -jnp.inf
    n_chunk = tb // CHUNK
    pairs = [(h, c) for c in range(n_chunk) for h in range(N_HEADS)]

    def head_cols(base, h):
        return y[:, base + h * HEAD_DIM: base + (h + 1) * HEAD_DIM]

    q_h = [_l2n(head_cols(0, h)) * (HEAD_DIM ** -0.5) for h in range(N_HEADS)]
    k_h = [_l2n(head_cols(BRANCH_WIDTH, h)) for h in range(N_HEADS)]
    v_h = [head_cols(2 * BRANCH_WIDTH, h) for h in range(N_HEADS)]

    rows = lambda c: slice(c * CHUNK, (c + 1) * CHUNK)
    q = {(h, c): q_h[h][rows(c)] for h, c in pairs}
    k = {(h, c): k_h[h][rows(c)] for h, c in pairs}
    v = {(h, c): v_h[h][rows(c)] for h, c in pairs}
    gc = {(h, c): gc_cols[rows(c), h:h + 1] for h, c in pairs}
    bt = {(h, c): beta_cols[rows(c), N_HEADS + h:N_HEADS + h + 1] for h, c in pairs}
    decay = {(h, c): jnp.exp(jnp.where(causal, gc[h, c] - gc_rows[h:h + 1, rows(c)], neg_inf)) for h, c in pairs}
    kb = {p: k[p] * bt[p] for p in pairs}
    L = {p: jnp.where(strict, _dot(kb[p], k[p], NT) * decay[p], 0.0) for p in pairs}
    qk = {p: jnp.where(causal, _dot(q[p], k[p], NT) * decay[p], 0.0) for p in pairs}
    Ld = {p: jnp.where(same_sub, L[p], 0.0) for p in pairs}
    Lo = {p: L[p] - Ld[p] for p in pairs}
    X = {p: eye - Ld[p] for p in pairs}
    P = {p: _dotp(Ld[p], Ld[p], GDN_PASSES["L2"]) for p in pairs}
    for it, (px, pp) in enumerate((("X2", "L4"), ("X4", "L8"), ("X8", None))):
        X = {p: X[p] + _dotp(X[p], P[p], GDN_PASSES[px]) for p in pairs}
        if pp is not None:
            P = {p: _dotp(P[p], P[p], GDN_PASSES[pp]) for p in pairs}
    M = {p: _dotp(X[p], Lo[p], GDN_PASSES["M"]) for p in pairs}
    M2 = {p: _dotp(M[p], M[p], GDN_PASSES["M2"]) for p in pairs}
    Y = {p: (eye - M[p]) + _dotp(eye - M[p], M2[p], GDN_PASSES["Y"]) for p in pairs}
    Tm = {p: _dotp(Y[p], X[p], GDN_PASSES["T"]) for p in pairs}
    uw = {p: _dotp(Tm[p], jnp.concatenate([v[p] * bt[p], kb[p] * jnp.exp(gc[p])], axis=1), GDN_PASSES["UW"])
          for p in pairs}
    qe = {p: q[p] * jnp.exp(gc[p]) for p in pairs}
    g_last = {p: gc[p][CHUNK - 1:CHUNK] for p in pairs}
    kd = {p: k[p] * jnp.exp(g_last[p] - gc[p]) for p in pairs}

    S = [s_ref[h] for h in range(N_HEADS)]
    o_rows = []
    for c in range(n_chunk):
        v_new = [uw[h, c][:, :HEAD_DIM] - _dot(uw[h, c][:, HEAD_DIM:], S[h]) for h in range(N_HEADS)]
        o_c = [_dot(qe[h, c], S[h]) + _dot(qk[h, c], v_new[h]) for h in range(N_HEADS)]
        S = [S[h] * jnp.exp(g_last[h, c]) + _dot(kd[h, c], v_new[h], TN) for h in range(N_HEADS)]
        o_rows.append(o_c)
    for h in range(N_HEADS):
        s_ref[h] = S[h]
    ng = ng_ref[...]
    z = z_ref[...]
    outs = []
    for h in range(N_HEADS):
        o = jnp.concatenate([o_rows[c][h] for c in range(n_chunk)], axis=0) if n_chunk > 1 else o_rows[0][h]
        o = o * lax.rsqrt(jnp.mean(o * o, axis=-1, keepdims=True) + RMS_EPS) * ng
        outs.append((o * _silu(z[:, h * HEAD_DIM:(h + 1) * HEAD_DIM])).astype(o_ref.dtype))
    o_ref[...] = jnp.concatenate(outs, axis=1)


def _gdn(proj, ab_t, conv_w, a_log, dt_bias, norm_g, B, T):
    tb = _pick(T, (128, 64))
    nt = T // tb
    r8 = tb // SUBLANE
    w3 = 3 * BRANCH_WIDTH
    pad = lambda p: jnp.pad(p.astype(F32), (0, LANE - N_HEADS))
    hrow = jnp.stack([pad(a_log), pad(dt_bias)])
    bc = lambda p: jnp.broadcast_to(p.astype(F32)[:, None], (N_HEADS, LANE))
    hcol = jnp.concatenate([bc(a_log), bc(dt_bias)], axis=1)
    const = lambda shape: pl.BlockSpec(shape, lambda b, t: (0,) * len(shape))
    return pl.pallas_call(
        functools.partial(_gdn_kernel, tb=tb), grid=(B, nt),
        in_specs=[pl.BlockSpec((tb, w3), lambda b, t: (b * nt + t, COL_GDN_QKV // w3)),
                  pl.BlockSpec((SUBLANE, w3),
                               lambda b, t: (jnp.maximum((b * nt + t) * r8 - 1, 0), COL_GDN_QKV // w3)),
                  pl.BlockSpec((tb, BRANCH_WIDTH), lambda b, t: (b * nt + t, COL_GDN_Z // BRANCH_WIDTH)),
                  pl.BlockSpec((tb, LANE), lambda b, t: (b * nt + t, COL_AB // LANE)),
                  pl.BlockSpec((2 * N_HEADS, tb), lambda b, t: (0, b * nt + t)),
                  const((GDN_CONV, w3)), const((2, LANE)), const((N_HEADS, 2 * LANE)), const((1, HEAD_DIM))],
        out_specs=pl.BlockSpec((tb, BRANCH_WIDTH), lambda b, t: (b * nt + t, 0)),
        out_shape=jax.ShapeDtypeStruct((B * T, BRANCH_WIDTH), BF16),
        scratch_shapes=[pltpu.VMEM((N_HEADS, HEAD_DIM, HEAD_DIM), F32)],
        compiler_params=_cparams(("parallel", "arbitrary")), name="gated_deltanet")(
            proj, proj, proj, proj, ab_t, conv_w, hrow, hcol, norm_g.reshape(1, HEAD_DIM))


def _swa_kernel(q_ref, kc_ref, vc_ref, kp_ref, vp_ref, sink_ref, o_ref):
    has_prev = pl.program_id(1) > 0
    qi = lax.broadcasted_iota(jnp.int32, (SWA_BLOCK, SWA_BLOCK), 0)
    kj = lax.broadcasted_iota(jnp.int32, (SWA_BLOCK, SWA_BLOCK), 1)
    dist_c = (qi - kj).astype(F32)
    dist_p = dist_c + float(SWA_BLOCK)
    valid_c = qi >= kj
    valid_p = (kj > qi) & has_prev
    neg_inf = -jnp.inf
    group = SWA_Q_HEADS // SWA_KV_HEADS
    q_all = q_ref[...] * (SWA_HEAD_DIM ** -0.5)
    kc, vc, kp, vp = kc_ref[...], vc_ref[...], kp_ref[...], vp_ref[...]
    sinks = sink_ref[...]
    heads = range(SWA_Q_HEADS)
    ks = [slice((hq // group) * SWA_HEAD_DIM, (hq // group + 1) * SWA_HEAD_DIM) for hq in heads]
    slope = [2.0 ** (-8.0 * (hq + 1) / SWA_Q_HEADS) for hq in heads]
    q = [q_all[:, hq * SWA_HEAD_DIM:(hq + 1) * SWA_HEAD_DIM].astype(BF16) for hq in heads]
    kcb, kpb, vcb, vpb = kc.astype(BF16), kp.astype(BF16), vc.astype(BF16), vp.astype(BF16)
    s_c = [jnp.where(valid_c, _dot(q[h], kcb[:, ks[h]], NT) - slope[h] * dist_c, neg_inf) for h in heads]
    s_p = [jnp.where(valid_p, _dot(q[h], kpb[:, ks[h]], NT) - slope[h] * dist_p, neg_inf) for h in heads]
    sink = [sinks[h:h + 1, :1] for h in heads]
    m = [jnp.maximum(jnp.max(jnp.maximum(s_c[h], s_p[h]), axis=-1, keepdims=True), sink[h]) for h in heads]
    p_c = [jnp.exp(s_c[h] - m[h]) for h in heads]
    p_p = [jnp.exp(s_p[h] - m[h]) for h in heads]
    inv = [1.0 / (jnp.sum(p_c[h] + p_p[h], axis=-1, keepdims=True) + jnp.exp(sink[h] - m[h])) for h in heads]
    outs = [_dot(p_c[h] * inv[h], vcb[:, ks[h]]) + _dot(p_p[h] * inv[h], vpb[:, ks[h]]) for h in heads]
    o_ref[...] = jnp.concatenate(outs, axis=1).astype(o_ref.dtype)


def _swa(proj, sinks, B, T):
    nb = T // SWA_BLOCK
    qw = SWA_Q_HEADS * SWA_HEAD_DIM
    cur = lambda col: pl.BlockSpec((SWA_BLOCK, LANE), lambda b, n, col=col: (b * nb + n, col))
    prv = lambda col: pl.BlockSpec((SWA_BLOCK, LANE),
                                   lambda b, n, col=col: (b * nb + jnp.maximum(n - 1, 0), col))
    sink_b = jnp.broadcast_to(sinks.astype(F32)[:, None], (SWA_Q_HEADS, LANE))
    return pl.pallas_call(
        _swa_kernel, grid=(B, nb),
        in_specs=[pl.BlockSpec((SWA_BLOCK, qw), lambda b, n: (b * nb + n, COL_SWA_Q // qw)),
                  cur(COL_SWA_K // LANE), cur(COL_SWA_V // LANE), prv(COL_SWA_K // LANE), prv(COL_SWA_V // LANE),
                  pl.BlockSpec((SWA_Q_HEADS, LANE), lambda b, n: (0, 0))],
        out_specs=pl.BlockSpec((SWA_BLOCK, qw), lambda b, n: (b * nb + n, 0)),
        out_shape=jax.ShapeDtypeStruct((B * T, qw), BF16),
        compiler_params=_cparams(("parallel", "arbitrary")), name="swa")(
            proj, proj, proj, proj, proj, sink_b)


SRC_HG, SRC_QKV, SRC_Z, SRC_AB = 0, 4096, 7168, 8192
N_AB = 2 * N_HEADS
SWA_COLS = COL_AB - COL_SWA_Q


def _w_in_kernel(w_ref, o_ref, *, tk):
    def put(dst, src, n):
        o_ref[dst:dst + n, :] = w_ref[src:src + n, :].astype(BF16)

    put(COL_GDN_QKV, SRC_QKV, 3 * BRANCH_WIDTH)
    put(COL_HG_Q, SRC_HG, 4 * BRANCH_WIDTH)
    put(COL_GDN_Z, SRC_Z, BRANCH_WIDTH)
    put(COL_SWA_Q, SRC_AB + N_AB, SWA_COLS)
    put(COL_AB, SRC_AB, N_AB)
    o_ref[COL_AB + N_AB:COL_GATES, :] = jnp.zeros((COL_GATES - COL_AB - N_AB, tk), BF16)
    put(COL_GATES, SRC_AB + N_AB + SWA_COLS, o_ref.shape[0] - COL_GATES)


def _prep_w_in(w):
    L, d, n_src = w.shape
    n_out = COL_GATES + (n_src - SRC_AB - N_AB - SWA_COLS)
    tk = LANE
    return pl.pallas_call(
        functools.partial(_w_in_kernel, tk=tk), grid=(L, d // tk),
        in_specs=[pl.BlockSpec((None, n_src, tk), lambda l, i: (l, 0, i))],
        out_specs=pl.BlockSpec((None, n_out, tk), lambda l, i: (l, 0, i)),
        out_shape=jax.ShapeDtypeStruct((L, n_out, d), BF16),
        compiler_params=_cparams(("parallel", "parallel")), name="w_in_relayout")(jnp.swapaxes(w, 1, 2))


def kernel(x, ln_in_g, ln_in_b, hg_lb_logits, w_in, gdn_conv_w, gdn_a_log, gdn_dt_bias, hg_norm_g,
           gdn_norm_g, swa_sinks, w_branch, w_out, ln1_g, ln1_b, w_gate_up, w_down, ln2_g, ln2_b):
    B, T, D = x.shape
    M = B * T
    depth = w_in.shape[0]
    lb_all = jnp.cumsum(jax.nn.softmax(hg_lb_logits.astype(F32), axis=0), axis=0)
    lb_all = lb_all - lb_all[0]

    w_in16 = _prep_w_in(w_in)
    w_branch16, w_out16 = w_branch.astype(BF16), w_out.astype(BF16)
    w_gate_up16, w_down16 = w_gate_up.astype(BF16), w_down.astype(BF16)

    h32, h16 = _layer_norm(x.reshape(M, D), ln_in_g, ln_in_b)
    for l in range(depth):
        proj = _matmul(h16, w_in16, l, F32, "in_proj", w_is_nk=True, cols=(0, COL_GATES))
        gate_logits = _matmul(h16, w_in16, l, BF16, "in_proj_gates", w_is_nk=True,
                              cols=(COL_GATES, w_in16.shape[1]))
        ab_t = proj[:, COL_AB:COL_AB + N_AB].T
        o_a = _hgrn2(proj, lb_all[l], hg_norm_g[l], B, T)
        o_b = _gdn(proj, ab_t, gdn_conv_w[l], gdn_a_log[l], gdn_dt_bias[l], gdn_norm_g[l], B, T)
        o_c = _swa(proj, swa_sinks[l], B, T)
        merged = _merge(o_a, o_b, o_c, w_branch16, l, gate_logits, D)
        mix = _matmul(merged, w_out16, l, F32, "out_proj")
        h32, h16 = _layer_norm(mix, ln1_g[l], ln1_b[l], res=h32)
        ff = _glu_matmul(h16, w_gate_up16, l)
        ff = _matmul(ff, w_down16, l, F32, "ffn_down")
        h32, h16 = _layer_norm(ff, ln2_g[l], ln2_b[l], res=h32)
    return h32.reshape(B, T, D)
```

```python
import functools
import math

import jax
import jax.numpy as jnp
import numpy as np
from jax import lax
from jax.experimental import pallas as pl
from jax.experimental.pallas import tpu as pltpu

F32 = jnp.float32
BF16 = jnp.bfloat16

N_HEADS = 8
HEAD_DIM = 128
BRANCH_WIDTH = N_HEADS * HEAD_DIM
GDN_CONV = 4
SWA_Q_HEADS = 16
SWA_KV_HEADS = 2
SWA_HEAD_DIM = 64
SWA_BLOCK = 128
CHUNK = 64
SUB = 16
DEPTH = 2
DEEPNORM_ALPHA = (2 * DEPTH) ** 0.25
LN_EPS = 1e-5
RMS_EPS = 1e-6
L2_EPS = 1e-6
LOG2_E = math.log2(math.e)

COL_GDN_QKV = 0
COL_HG_Q, COL_HG_F, COL_HG_I, COL_HG_G = 3072, 4096, 5120, 6144
COL_GDN_Z = 7168
COL_SWA_Q, COL_SWA_K, COL_SWA_V = 8192, 9216, 9344
COL_AB = 9472
COL_GATES = 10240
LANE = 128
SUBLANE = 8

VMEM_LIMIT = 56 * 1024 * 1024


def _cparams(sem):
    return pltpu.CompilerParams(dimension_semantics=sem, vmem_limit_bytes=VMEM_LIMIT)


def _pick(n, cands):
    for c in cands:
        if n % c == 0:
            return c
    raise ValueError(f"no tile for {n} in {cands}")


def _sigmoid(x):
    return 1.0 / (1.0 + jnp.exp(-x))


def _silu(x):
    return x * _sigmoid(x)


def _log_sigmoid(x):
    return jnp.minimum(x, 0.0) - jnp.log1p(jnp.exp(-jnp.abs(x)))


def _softplus(x):
    return jnp.maximum(x, 0.0) + jnp.log1p(jnp.exp(-jnp.abs(x)))


def _logaddexp(a, b):
    return jnp.maximum(a, b) + jnp.log1p(jnp.exp(-jnp.abs(a - b)))


NN = (((1,), (0,)), ((), ()))
NT = (((1,), (1,)), ((), ()))
TN = (((0,), (0,)), ((), ()))


def _dot(a, b, dims=NN):
    return lax.dot_general(a.astype(BF16), b.astype(BF16), dims, preferred_element_type=F32)


def _split2(x):
    hi = x.astype(BF16)
    lo = (x - hi.astype(F32)).astype(BF16)
    return hi, lo


def _dot3s(a, b, dims=NN):
    dg = functools.partial(lax.dot_general, dimension_numbers=dims, preferred_element_type=F32)
    return dg(a[0], b[0]) + (dg(a[0], b[1]) + dg(a[1], b[0]))


def _dot3(a, b, dims=NN):
    return _dot3s(_split2(a), _split2(b), dims)


def _dotp(a, b, passes):
    return _dot3(a, b) if passes == 3 else _dot(a, b)


GDN_PASSES = {"L2": 3, "X2": 3, "L4": 1, "X4": 1, "L8": 1, "X8": 1, "M": 3, "M2": 1, "Y": 1, "T": 1, "UW": 1}


def _split3(x):
    hi = x.astype(BF16)
    r1 = x - hi.astype(F32)
    mid = r1.astype(BF16)
    lo = (r1 - mid.astype(F32)).astype(BF16)
    return hi, mid, lo


def _dot_exact_lhs(m_bf16, x):
    hi, mid, lo = _split3(x)
    dg = functools.partial(lax.dot_general, dimension_numbers=NN, preferred_element_type=F32)
    return dg(m_bf16, hi) + (dg(m_bf16, mid) + dg(m_bf16, lo))


def _dot_exact_rhs(x, m_bf16):
    hi, mid, lo = _split3(x)
    dg = functools.partial(lax.dot_general, dimension_numbers=NN, preferred_element_type=F32)
    return dg(hi, m_bf16) + (dg(mid, m_bf16) + dg(lo, m_bf16))


def _block_tri(n, lower):
    r = lax.broadcasted_iota(jnp.int32, (n, n), 0)
    c = lax.broadcasted_iota(jnp.int32, (n, n), 1)
    same = (r // CHUNK) == (c // CHUNK)
    tri = (r >= c) if lower else (r <= c)
    return jnp.where(same & tri, 1.0, 0.0).astype(BF16)


def _ln_core(x, g, b):
    mu = jnp.mean(x, axis=-1, keepdims=True)
    xc = x - mu
    var = jnp.mean(xc * xc, axis=-1, keepdims=True)
    return xc * lax.rsqrt(var + LN_EPS) * g + b


def _ln_kernel(x_ref, g_ref, b_ref, o32_ref, o16_ref):
    y = _ln_core(x_ref[...], g_ref[...], b_ref[...])
    o32_ref[...] = y
    o16_ref[...] = y.astype(BF16)


def _ln_res_kernel(h_ref, y_ref, g_ref, b_ref, o32_ref, o16_ref):
    y = _ln_core(DEEPNORM_ALPHA * h_ref[...] + y_ref[...].astype(F32), g_ref[...], b_ref[...])
    o32_ref[...] = y
    o16_ref[...] = y.astype(BF16)


def _layer_norm(x, g, b, res=None):
    M, D = x.shape
    tm = _pick(M, (256, 128, 64, 32, 16))
    row = pl.BlockSpec((tm, D), lambda i: (i, 0))
    vec = pl.BlockSpec((1, D), lambda i: (0, 0))
    g2, b2 = g.reshape(1, D), b.reshape(1, D)
    out_shape = (jax.ShapeDtypeStruct((M, D), F32), jax.ShapeDtypeStruct((M, D), BF16))
    if res is None:
        return pl.pallas_call(_ln_kernel, grid=(M // tm,), in_specs=[row, vec, vec],
                              out_specs=(row, row), out_shape=out_shape,
                              compiler_params=_cparams(("parallel",)), name="layer_norm")(x, g2, b2)
    return pl.pallas_call(_ln_res_kernel, grid=(M // tm,), in_specs=[row, row, vec, vec],
                          out_specs=(row, row), out_shape=out_shape,
                          compiler_params=_cparams(("parallel",)), name="layer_norm_res")(res, x, g2, b2)


def _mm_kernel(a_ref, w_ref, o_ref):
    o_ref[...] = jnp.dot(a_ref[...], w_ref[...], preferred_element_type=F32).astype(o_ref.dtype)


def _mm_nt_kernel(a_ref, w_ref, o_ref):
    o_ref[...] = lax.dot_general(a_ref[...], w_ref[...], NT, preferred_element_type=F32).astype(o_ref.dtype)


def _matmul(a, w, l, out_dtype, name, w_is_nk=False, cols=None):
    M, K = a.shape
    n_all = w.shape[1] if w_is_nk else w.shape[2]
    c0, c1 = cols if cols is not None else (0, n_all)
    N = c1 - c0
    tm = _pick(M, (1024, 512, 256, 128))
    tn = _pick(math.gcd(N, c0) if c0 else N, (1024, 512, 256, 128))
    if K > 8192:
        tm, tn = min(tm, 512), min(tn, 512)
    j0 = c0 // tn
    w_spec = (pl.BlockSpec((None, tn, K), lambda i, j: (l, j0 + j, 0)) if w_is_nk
              else pl.BlockSpec((None, K, tn), lambda i, j: (l, 0, j0 + j)))
    return pl.pallas_call(
        _mm_nt_kernel if w_is_nk else _mm_kernel, grid=(M // tm, N // tn),
        in_specs=[pl.BlockSpec((tm, K), lambda i, j: (i, 0)), w_spec],
        out_specs=pl.BlockSpec((tm, tn), lambda i, j: (i, j)),
        out_shape=jax.ShapeDtypeStruct((M, N), out_dtype),
        compiler_params=_cparams(("parallel", "parallel")), name=name)(a, w)


def _glu_kernel(a_ref, wg_ref, wu_ref, o_ref):
    a = a_ref[...]
    g = jnp.dot(a, wg_ref[...], preferred_element_type=F32)
    u = jnp.dot(a, wu_ref[...], preferred_element_type=F32)
    o_ref[...] = (_silu(g) * u).astype(o_ref.dtype)


def _glu_matmul(a, w_gate_up, l):
    M, K = a.shape
    F = w_gate_up.shape[2] // 2
    tm = _pick(M, (2048, 1024, 512, 256, 128))
    tn = _pick(F, (512, 256, 128))
    nf = F // tn
    return pl.pallas_call(
        _glu_kernel, grid=(M // tm, nf),
        in_specs=[pl.BlockSpec((tm, K), lambda i, j: (i, 0)),
                  pl.BlockSpec((None, K, tn), lambda i, j: (l, 0, j)),
                  pl.BlockSpec((None, K, tn), lambda i, j: (l, 0, j + nf))],
        out_specs=pl.BlockSpec((tm, tn), lambda i, j: (i, j)),
        out_shape=jax.ShapeDtypeStruct((M, F), BF16),
        compiler_params=_cparams(("parallel", "parallel")), name="ffn_gate_up")(a, w_gate_up, w_gate_up)


def _merge_kernel(oa_ref, ob_ref, oc_ref, wb_ref, g0_ref, g1_ref, g2_ref, o_ref):
    gate = lambda g_ref: _sigmoid(g_ref[...].astype(F32))
    acc = gate(g0_ref) * jnp.dot(oa_ref[...], wb_ref[0], preferred_element_type=F32)
    acc = acc + gate(g1_ref) * jnp.dot(ob_ref[...], wb_ref[1], preferred_element_type=F32)
    acc = acc + gate(g2_ref) * jnp.dot(oc_ref[...], wb_ref[2], preferred_element_type=F32)
    o_ref[...] = acc.astype(o_ref.dtype)


def _merge(o_a, o_b, o_c, w_branch, l, gate_logits, d_model):
    M = o_a.shape[0]
    tm = _pick(M, (1024, 512, 256, 128))
    tn = _pick(d_model, (512, 256, 128))
    g_step = d_model // tn
    o_spec = pl.BlockSpec((tm, BRANCH_WIDTH), lambda i, j: (i, 0))
    g_specs = [pl.BlockSpec((tm, tn), functools.partial(lambda i, j, b: (i, b * g_step + j), b=b))
               for b in range(3)]
    return pl.pallas_call(
        _merge_kernel, grid=(M // tm, d_model // tn),
        in_specs=[o_spec, o_spec, o_spec,
                  pl.BlockSpec((None, 3, BRANCH_WIDTH, tn), lambda i, j: (l, 0, 0, j))] + g_specs,
        out_specs=pl.BlockSpec((tm, tn), lambda i, j: (i, j)),
        out_shape=jax.ShapeDtypeStruct((M, d_model), BF16),
        compiler_params=_cparams(("parallel", "parallel")), name="branch_merge")(
            o_a, o_b, o_c, w_branch, gate_logits, gate_logits, gate_logits)


def _hgrn2_kernel(q_ref, f_ref, i_ref, g_ref, lb_ref, ng_ref, e_ref, o_ref, st_ref, *, tb):
    @pl.when(pl.program_id(1) == 0)
    def _():
        st_ref[...] = jnp.zeros_like(st_ref)

    lb = lb_ref[...]
    log_lb = jnp.log(lb)
    log1m_lb = jnp.log1p(-lb)
    z = f_ref[...]
    log_f = _logaddexp(log_lb, log1m_lb + _log_sigmoid(z))
    k_all = (1.0 - lb) * _sigmoid(-z)
    q_all = _silu(q_ref[...])
    v_all = i_ref[...]
    g_all = _dot_exact_lhs(_block_tri(tb, True), log_f) * LOG2_E

    row8 = lax.broadcasted_iota(jnp.int32, (SUBLANE, HEAD_DIM), 0)
    same_sub = (lax.broadcasted_iota(jnp.int32, (CHUNK, CHUNK), 0) // SUB
                == lax.broadcasted_iota(jnp.int32, (CHUNK, CHUNK), 1) // SUB)
    e_mat = e_ref[...]
    n_sub = CHUNK // SUB
    n_chunk = tb // CHUNK
    neg_inf = -jnp.inf
    zeros8 = jnp.zeros((SUBLANE, HEAD_DIM), F32)
    pairs = [(h, c) for c in range(n_chunk) for h in range(N_HEADS)]

    def part(x, h, c):
        return x[c * CHUNK:(c + 1) * CHUNK, h * HEAD_DIM:(h + 1) * HEAD_DIM]

    G = {p: part(g_all, *p) for p in pairs}
    q = {p: part(q_all, *p) for p in pairs}
    k = {p: part(k_all, *p) for p in pairs}
    v = {p: part(v_all, *p) for p in pairs}

    def diag_products(G, q, k):
        pcs = []
        for I in range(n_sub):
            s0 = I * SUB
            g_top, g_bot = G[s0:s0 + SUBLANE], G[s0 + SUBLANE:s0 + SUB]
            q_top, q_bot = q[s0:s0 + SUBLANE], q[s0 + SUBLANE:s0 + SUB]
            cols = []
            for j in range(SUB):
                r = s0 + j
                g_r, k_r = G[r:r + 1], k[r:r + 1]
                if j < SUBLANE:
                    top = q_top * jnp.exp2(jnp.where(row8 >= j, g_top - g_r, neg_inf)) * k_r
                    bot = q_bot * jnp.exp2(g_bot - g_r) * k_r
                else:
                    top = zeros8
                    bot = q_bot * jnp.exp2(jnp.where(row8 >= j - SUBLANE, g_bot - g_r, neg_inf)) * k_r
                cols.append(jnp.concatenate([top, bot], axis=0).astype(BF16))
            pcs.append(jnp.concatenate(cols, axis=1))
        return jnp.concatenate(pcs, axis=0)

    pcat = {p: diag_products(G[p], q[p], k[p]) for p in pairs}
    dfull = {p: jnp.where(same_sub, jnp.dot(pcat[p], e_mat, preferred_element_type=F32), 0.0) for p in pairs}

    def below_diag(G, q, k, dfull):
        a_rows = [dfull[0:SUB]]
        for I in range(1, n_sub):
            s0 = I * SUB
            gb = G[s0 - 1:s0]
            qt = q[s0:s0 + SUB] * jnp.exp2(G[s0:s0 + SUB] - gb)
            kx = jnp.concatenate([k[:s0] * jnp.exp2(gb - G[:s0]), jnp.zeros((CHUNK - s0, HEAD_DIM), F32)], axis=0)
            a_rows.append(_dot(qt, kx, NT) + dfull[s0:s0 + SUB])
        return jnp.concatenate(a_rows, axis=0)

    a_mat = {p: below_diag(G[p], q[p], k[p], dfull[p]) for p in pairs}
    o_intra = {p: _dot(a_mat[p], v[p]) for p in pairs}
    qe = {p: q[p] * jnp.exp2(G[p]) for p in pairs}
    g_last = {p: G[p][CHUNK - 1:CHUNK] for p in pairs}
    kd = {p: k[p] * jnp.exp2(g_last[p] - G[p]) for p in pairs}

    st = [st_ref[h] for h in range(N_HEADS)]
    o_rows = []
    for c in range(n_chunk):
        o_rows.append([o_intra[h, c] + _dot(qe[h, c], st[h], NT) for h in range(N_HEADS)])
        st = [st[h] * jnp.exp2(g_last[h, c]) + _dot(v[h, c], kd[h, c], TN) for h in range(N_HEADS)]
    for h in range(N_HEADS):
        st_ref[h] = st[h]
    ng = ng_ref[...]
    gate = _silu(g_ref[...])
    outs = []
    for h in range(N_HEADS):
        o = jnp.concatenate([o_rows[c][h] for c in range(n_chunk)], axis=0) if n_chunk > 1 else o_rows[0][h]
        o = o * lax.rsqrt(jnp.mean(o * o, axis=-1, keepdims=True) + RMS_EPS) * ng
        outs.append((o * gate[:, h * HEAD_DIM:(h + 1) * HEAD_DIM]).astype(o_ref.dtype))
    o_ref[...] = jnp.concatenate(outs, axis=1)


def _hgrn2_emat():
    e = np.zeros((SUB * HEAD_DIM, CHUNK), np.float32)
    for j in range(SUB):
        for I in range(CHUNK // SUB):
            e[j * HEAD_DIM:(j + 1) * HEAD_DIM, I * SUB + j] = 1.0
    return jnp.asarray(e, BF16)


def _hgrn2(proj, lb, norm_g, B, T):
    tb = _pick(T, (128, 64))
    nt = T // tb
    cb = lambda col: pl.BlockSpec((tb, BRANCH_WIDTH), lambda b, t, col=col: (b * nt + t, col // BRANCH_WIDTH))
    const = lambda shape: pl.BlockSpec(shape, lambda b, t: (0,) * len(shape))
    return pl.pallas_call(
        functools.partial(_hgrn2_kernel, tb=tb), grid=(B, nt),
        in_specs=[cb(COL_HG_Q), cb(COL_HG_F), cb(COL_HG_I), cb(COL_HG_G),
                  const((1, BRANCH_WIDTH)), const((1, HEAD_DIM)), const((SUB * HEAD_DIM, CHUNK))],
        out_specs=pl.BlockSpec((tb, BRANCH_WIDTH), lambda b, t: (b * nt + t, 0)),
        out_shape=jax.ShapeDtypeStruct((B * T, BRANCH_WIDTH), BF16),
        scratch_shapes=[pltpu.VMEM((N_HEADS, HEAD_DIM, HEAD_DIM), F32)],
        compiler_params=_cparams(("parallel", "arbitrary")), name="hgrn2")(
            proj, proj, proj, proj, lb.reshape(1, BRANCH_WIDTH), norm_g.reshape(1, HEAD_DIM), _hgrn2_emat())


def _l2n(x):
    return x * lax.rsqrt(jnp.sum(x * x, axis=-1, keepdims=True) + L2_EPS)


def _gdn_kernel(qkv_ref, prev_ref, z_ref, ab_ref, abt_ref, w_ref, hrow_ref, hcol_ref, ng_ref,
                o_ref, s_ref, *, tb):
    first = pl.program_id(1) == 0

    @pl.when(first)
    def _():
        s_ref[...] = jnp.zeros_like(s_ref)

    x = qkv_ref[...]
    prev = jnp.where(first, 0.0, prev_ref[...])
    xf = jnp.concatenate([prev, x], axis=0)
    w = w_ref[...]
    acc = x * w[GDN_CONV - 1:GDN_CONV]
    for j in range(GDN_CONV - 1):
        off = SUBLANE - (GDN_CONV - 1) + j
        acc = acc + xf[off:off + tb] * w[j:j + 1]
    y = _silu(acc)

    hrow = hrow_ref[...]
    hcol = hcol_ref[...]
    ab = ab_ref[...]
    g_cols = -jnp.exp(hrow[0:1]) * _softplus(ab + hrow[1:2])
    gc_cols = _dot_exact_lhs(_block_tri(tb, True), g_cols)
    beta_cols = _sigmoid(ab)
    abt = abt_ref[...]
    g_rows = -jnp.exp(hcol[:, 0:1]) * _softplus(abt[0:N_HEADS] + hcol[:, LANE:LANE + 1])
    gc_rows = _dot_exact_rhs(g_rows, _block_tri(tb, False))

    ri = lax.broadcasted_iota(jnp.int32, (CHUNK, CHUNK), 0)
    ci = lax.broadcasted_iota(jnp.int32, (CHUNK, CHUNK), 1)
    causal = ri >= ci
    strict = ri > ci
    same_sub = (ri // SUB) == (ci // SUB)
    eye = jnp.where(ri == ci, 1.0, 0.0)
    neg_inf = -jnp.inf
    n_chunk = tb // CHUNK
    pairs = [(h, c) for c in range(n_chunk) for h in range(N_HEADS)]

    def head_cols(base, h):
        return y[:, base + h * HEAD_DIM: base + (h + 1) * HEAD_DIM]

    q_h = [_l2n(head_cols(0, h)) * (HEAD_DIM ** -0.5) for h in range(N_HEADS)]
    k_h = [_l2n(head_cols(BRANCH_WIDTH, h)) for h in range(N_HEADS)]
    v_h = [head_cols(2 * BRANCH_WIDTH, h) for h in range(N_HEADS)]

    rows = lambda c: slice(c * CHUNK, (c + 1) * CHUNK)
    q = {(h, c): q_h[h][rows(c)] for h, c in pairs}
    k = {(h, c): k_h[h][rows(c)] for h, c in pairs}
    v = {(h, c): v_h[h][rows(c)] for h, c in pairs}
    gc = {(h, c): gc_cols[rows(c), h:h + 1] for h, c in pairs}
    bt = {(h, c): beta_cols[rows(c), N_HEADS + h:N_HEADS + h + 1] for h, c in pairs}
    decay = {(h, c): jnp.exp(jnp.where(causal, gc[h, c] - gc_rows[h:h + 1, rows(c)], neg_inf)) for h, c in pairs}
    kb = {p: k[p] * bt[p] for p in pairs}
    L = {p: jnp.where(strict, _dot(kb[p], k[p], NT) * decay[p], 0.0) for p in pairs}
    qk = {p: jnp.where(causal, _dot(q[p], k[p], NT) * decay[p], 0.0) for p in pairs}
    Ld = {p: jnp.where(same_sub, L[p], 0.0) for p in pairs}
    Lo = {p: L[p] - Ld[p] for p in pairs}
    X = {p: eye - Ld[p] for p in pairs}
    P = {p: _dotp(Ld[p], Ld[p], GDN_PASSES["L2"]) for p in pairs}
    for it, (px, pp) in enumerate((("X2", "L4"), ("X4", "L8"), ("X8", None))):
        X = {p: X[p] + _dotp(X[p], P[p], GDN_PASSES[px]) for p in pairs}
        if pp is not None:
            P = {p: _dotp(P[p], P[p], GDN_PASSES[pp]) for p in pairs}
    M = {p: _dotp(X[p], Lo[p], GDN_PASSES["M"]) for p in pairs}
    M2 = {p: _dotp(M[p], M[p], GDN_PASSES["M2"]) for p in pairs}
    Y = {p: (eye - M[p]) + _dotp(eye - M[p], M2[p], GDN_PASSES["Y"]) for p in pairs}
    Tm = {p: _dotp(Y[p], X[p], GDN_PASSES["T"]) for p in pairs}
    uw = {p: _dotp(Tm[p], jnp.concatenate([v[p] * bt[p], kb[p] * jnp.exp(gc[p])], axis=1), GDN_PASSES["UW"])
          for p in pairs}
    qe = {p: q[p] * jnp.exp(gc[p]) for p in pairs}
    g_last = {p: gc[p][CHUNK - 1:CHUNK] for p in pairs}
    kd = {p: k[p] * jnp.exp(g_last[p] - gc[p]) for p in pairs}

    S = [s_ref[h] for h in range(N_HEADS)]
    o_rows = []
    for c in range(n_chunk):
        v_new = [uw[h, c][:, :HEAD_DIM] - _dot(uw[h, c][:, HEAD_DIM:], S[h]) for h in range(N_HEADS)]
        o_c = [_dot(qe[h, c], S[h]) + _dot(qk[h, c], v_new[h]) for h in range(N_HEADS)]
        S = [S[h] * jnp.exp(g_last[h, c]) + _dot(kd[h, c], v_new[h], TN) for h in range(N_HEADS)]
        o_rows.append(o_c)
    for h in range(N_HEADS):
        s_ref[h] = S[h]
    ng = ng_ref[...]
    z = z_ref[...]
    outs = []
    for h in range(N_HEADS):
        o = jnp.concatenate([o_rows[c][h] for c in range(n_chunk)], axis=0) if n_chunk > 1 else o_rows[0][h]
        o = o * lax.rsqrt(jnp.mean(o * o, axis=-1, keepdims=True) + RMS_EPS) * ng
        outs.append((o * _silu(z[:, h * HEAD_DIM:(h + 1) * HEAD_DIM])).astype(o_ref.dtype))
    o_ref[...] = jnp.concatenate(outs, axis=1)


def _gdn(proj, ab_t, conv_w, a_log, dt_bias, norm_g, B, T):
    tb = _pick(T, (128, 64))
    nt = T // tb
    r8 = tb // SUBLANE
    w3 = 3 * BRANCH_WIDTH
    pad = lambda p: jnp.pad(p.astype(F32), (0, LANE - N_HEADS))
    hrow = jnp.stack([pad(a_log), pad(dt_bias)])
    bc = lambda p: jnp.broadcast_to(p.astype(F32)[:, None], (N_HEADS, LANE))
    hcol = jnp.concatenate([bc(a_log), bc(dt_bias)], axis=1)
    const = lambda shape: pl.BlockSpec(shape, lambda b, t: (0,) * len(shape))
    return pl.pallas_call(
        functools.partial(_gdn_kernel, tb=tb), grid=(B, nt),
        in_specs=[pl.BlockSpec((tb, w3), lambda b, t: (b * nt + t, COL_GDN_QKV // w3)),
                  pl.BlockSpec((SUBLANE, w3),
                               lambda b, t: (jnp.maximum((b * nt + t) * r8 - 1, 0), COL_GDN_QKV // w3)),
                  pl.BlockSpec((tb, BRANCH_WIDTH), lambda b, t: (b * nt + t, COL_GDN_Z // BRANCH_WIDTH)),
                  pl.BlockSpec((tb, LANE), lambda b, t: (b * nt + t, COL_AB // LANE)),
                  pl.BlockSpec((2 * N_HEADS, tb), lambda b, t: (0, b * nt + t)),
                  const((GDN_CONV, w3)), const((2, LANE)), const((N_HEADS, 2 * LANE)), const((1, HEAD_DIM))],
        out_specs=pl.BlockSpec((tb, BRANCH_WIDTH), lambda b, t: (b * nt + t, 0)),
        out_shape=jax.ShapeDtypeStruct((B * T, BRANCH_WIDTH), BF16),
        scratch_shapes=[pltpu.VMEM((N_HEADS, HEAD_DIM, HEAD_DIM), F32)],
        compiler_params=_cparams(("parallel", "arbitrary")), name="gated_deltanet")(
            proj, proj, proj, proj, ab_t, conv_w, hrow, hcol, norm_g.reshape(1, HEAD_DIM))


def _swa_kernel(q_ref, kc_ref, vc_ref, kp_ref, vp_ref, sink_ref, o_ref):
    has_prev = pl.program_id(1) > 0
    qi = lax.broadcasted_iota(jnp.int32, (SWA_BLOCK, SWA_BLOCK), 0)
    kj = lax.broadcasted_iota(jnp.int32, (SWA_BLOCK, SWA_BLOCK), 1)
    dist_c = (qi - kj).astype(F32)
    dist_p = dist_c + float(SWA_BLOCK)
    valid_c = qi >= kj
    valid_p = (kj > qi) & has_prev
    neg_inf = -jnp.inf
    group = SWA_Q_HEADS // SWA_KV_HEADS
    q_all = q_ref[...] * (SWA_HEAD_DIM ** -0.5)
    kc, vc, kp, vp = kc_ref[...], vc_ref[...], kp_ref[...], vp_ref[...]
    sinks = sink_ref[...]
    heads = range(SWA_Q_HEADS)
    ks = [slice((hq // group) * SWA_HEAD_DIM, (hq // group + 1) * SWA_HEAD_DIM) for hq in heads]
    slope = [2.0 ** (-8.0 * (hq + 1) / SWA_Q_HEADS) for hq in heads]
    q = [q_all[:, hq * SWA_HEAD_DIM:(hq + 1) * SWA_HEAD_DIM].astype(BF16) for hq in heads]
    kcb, kpb, vcb, vpb = kc.astype(BF16), kp.astype(BF16), vc.astype(BF16), vp.astype(BF16)
    s_c = [jnp.where(valid_c, _dot(q[h], kcb[:, ks[h]], NT) - slope[h] * dist_c, neg_inf) for h in heads]
    s_p = [jnp.where(valid_p, _dot(q[h], kpb[:, ks[h]], NT) - slope[h] * dist_p, neg_inf) for h in heads]
    sink = [sinks[h:h + 1, :1] for h in heads]
    m = [jnp.maximum(jnp.max(jnp.maximum(s_c[h], s_p[h]), axis=-1, keepdims=True), sink[h]) for h in heads]
    p_c = [jnp.exp(s_c[h] - m[h]) for h in heads]
    p_p = [jnp.exp(s_p[h] - m[h]) for h in heads]
    inv = [1.0 / (jnp.sum(p_c[h] + p_p[h], axis=-1, keepdims=True) + jnp.exp(sink[h] - m[h])) for h in heads]
    outs = [_dot(p_c[h] * inv[h], vcb[:, ks[h]]) + _dot(p_p[h] * inv[h], vpb[:, ks[h]]) for h in heads]
    o_ref[...] = jnp.concatenate(outs, axis=1).astype(o_ref.dtype)


def _swa(proj, sinks, B, T):
    nb = T // SWA_BLOCK
    qw = SWA_Q_HEADS * SWA_HEAD_DIM
    cur = lambda col: pl.BlockSpec((SWA_BLOCK, LANE), lambda b, n, col=col: (b * nb + n, col))
    prv = lambda col: pl.BlockSpec((SWA_BLOCK, LANE),
                                   lambda b, n, col=col: (b * nb + jnp.maximum(n - 1, 0), col))
    sink_b = jnp.broadcast_to(sinks.astype(F32)[:, None], (SWA_Q_HEADS, LANE))
    return pl.pallas_call(
        _swa_kernel, grid=(B, nb),
        in_specs=[pl.BlockSpec((SWA_BLOCK, qw), lambda b, n: (b * nb + n, COL_SWA_Q // qw)),
                  cur(COL_SWA_K // LANE), cur(COL_SWA_V // LANE), prv(COL_SWA_K // LANE), prv(COL_SWA_V // LANE),
                  pl.BlockSpec((SWA_Q_HEADS, LANE), lambda b, n: (0, 0))],
        out_specs=pl.BlockSpec((SWA_BLOCK, qw), lambda b, n: (b * nb + n, 0)),
        out_shape=jax.ShapeDtypeStruct((B * T, qw), BF16),
        compiler_params=_cparams(("parallel", "arbitrary")), name="swa")(
            proj, proj, proj, proj, proj, sink_b)


SRC_HG, SRC_QKV, SRC_Z, SRC_AB = 0, 4096, 7168, 8192
N_AB = 2 * N_HEADS
SWA_COLS = COL_AB - COL_SWA_Q


def _w_in_kernel(w_ref, o_ref, *, tk):
    def put(dst, src, n):
        o_ref[dst:dst + n, :] = w_ref[src:src + n, :].astype(BF16)

    put(COL_GDN_QKV, SRC_QKV, 3 * BRANCH_WIDTH)
    put(COL_HG_Q, SRC_HG, 4 * BRANCH_WIDTH)
    put(COL_GDN_Z, SRC_Z, BRANCH_WIDTH)
    put(COL_SWA_Q, SRC_AB + N_AB, SWA_COLS)
    put(COL_AB, SRC_AB, N_AB)
    o_ref[COL_AB + N_AB:COL_GATES, :] = jnp.zeros((COL_GATES - COL_AB - N_AB, tk), BF16)
    put(COL_GATES, SRC_AB + N_AB + SWA_COLS, o_ref.shape[0] - COL_GATES)


def _prep_w_in(w):
    L, d, n_src = w.shape
    n_out = COL_GATES + (n_src - SRC_AB - N_AB - SWA_COLS)
    tk = LANE
    return pl.pallas_call(
        functools.partial(_w_in_kernel, tk=tk), grid=(L, d // tk),
        in_specs=[pl.BlockSpec((None, n_src, tk), lambda l, i: (l, 0, i))],
        out_specs=pl.BlockSpec((None, n_out, tk), lambda l, i: (l, 0, i)),
        out_shape=jax.ShapeDtypeStruct((L, n_out, d), BF16),
        compiler_params=_cparams(("parallel", "parallel")), name="w_in_relayout")(jnp.swapaxes(w, 1, 2))


def kernel(x, ln_in_g, ln_in_b, hg_lb_logits, w_in, gdn_conv_w, gdn_a_log, gdn_dt_bias, hg_norm_g,
           gdn_norm_g, swa_sinks, w_branch, w_out, ln1_g, ln1_b, w_gate_up, w_down, ln2_g, ln2_b):
    B, T, D = x.shape
    M = B * T
    depth = w_in.shape[0]
    lb_all = jnp.cumsum(jax.nn.softmax(hg_lb_logits.astype(F32), axis=0), axis=0)
    lb_all = lb_all - lb_all[0]

    w_in16 = _prep_w_in(w_in)
    w_branch16, w_out16 = w_branch.astype(BF16), w_out.astype(BF16)
    w_gate_up16, w_down16 = w_gate_up.astype(BF16), w_down.astype(BF16)

    h32, h16 = _layer_norm(x.reshape(M, D), ln_in_g, ln_in_b)
    for l in range(depth):
        proj = _matmul(h16, w_in16, l, F32, "in_proj", w_is_nk=True, cols=(0, COL_GATES))
        gate_logits = _matmul(h16, w_in16, l, BF16, "in_proj_gates", w_is_nk=True,
                              cols=(COL_GATES, w_in16.shape[1]))
        ab_t = proj[:, COL_AB:COL_AB + N_AB].T
        o_a = _hgrn2(proj, lb_all[l], hg_norm_g[l], B, T)
        o_b = _gdn(proj, ab_t, gdn_conv_w[l], gdn_a_log[l], gdn_dt_bias[l], gdn_norm_g[l], B, T)
        o_c = _swa(proj, swa_sinks[l], B, T)
        merged = _merge(o_a, o_b, o_c, w_branch16, l, gate_logits, D)
        mix = _matmul(merged, w_out16, l, BF16, "out_proj")
        h32, h16 = _layer_norm(mix, ln1_g[l], ln1_b[l], res=h32)
        ff = _glu_matmul(h16, w_gate_up16, l)
        ff = _matmul(ff, w_down16, l, BF16, "ffn_down")
        h32, h16 = _layer_norm(ff, ln2_g[l], ln2_b[l], res=h32)
    return h32.reshape(B, T, D)
```

```python
import functools
import math

import jax
import jax.numpy as jnp
import numpy as np
from jax import lax
from jax.experimental import pallas as pl
from jax.experimental.pallas import tpu as pltpu

F32 = jnp.float32
BF16 = jnp.bfloat16

N_HEADS = 8
HEAD_DIM = 128
BRANCH_WIDTH = N_HEADS * HEAD_DIM
GDN_CONV = 4
SWA_Q_HEADS = 16
SWA_KV_HEADS = 2
SWA_HEAD_DIM = 64
SWA_BLOCK = 128
CHUNK = 64
SUB = 16
DEPTH = 2
DEEPNORM_ALPHA = (2 * DEPTH) ** 0.25
LN_EPS = 1e-5
RMS_EPS = 1e-6
L2_EPS = 1e-6
LOG2_E = math.log2(math.e)

COL_GDN_QKV = 0
COL_HG_Q, COL_HG_F, COL_HG_I, COL_HG_G = 3072, 4096, 5120, 6144
COL_GDN_Z = 7168
COL_SWA_Q, COL_SWA_K, COL_SWA_V = 8192, 9216, 9344
COL_AB = 9472
COL_GATES = 10240
LANE = 128
SUBLANE = 8

VMEM_LIMIT = 56 * 1024 * 1024


def _cparams(sem):
    return pltpu.CompilerParams(dimension_semantics=sem, vmem_limit_bytes=VMEM_LIMIT)


def _pick(n, cands):
    for c in cands:
        if n % c == 0:
            return c
    raise ValueError(f"no tile for {n} in {cands}")


def _sigmoid(x):
    return 1.0 / (1.0 + jnp.exp(-x))


def _silu(x):
    return x * _sigmoid(x)


def _log_sigmoid(x):
    return jnp.minimum(x, 0.0) - jnp.log1p(jnp.exp(-jnp.abs(x)))


def _softplus(x):
    return jnp.maximum(x, 0.0) + jnp.log1p(jnp.exp(-jnp.abs(x)))


def _logaddexp(a, b):
    return jnp.maximum(a, b) + jnp.log1p(jnp.exp(-jnp.abs(a - b)))


NN = (((1,), (0,)), ((), ()))
NT = (((1,), (1,)), ((), ()))
TN = (((0,), (0,)), ((), ()))


def _dot(a, b, dims=NN):
    return lax.dot_general(a.astype(BF16), b.astype(BF16), dims, preferred_element_type=F32)


def _split2(x):
    hi = x.astype(BF16)
    lo = (x - hi.astype(F32)).astype(BF16)
    return hi, lo


def _dot3s(a, b, dims=NN):
    dg = functools.partial(lax.dot_general, dimension_numbers=dims, preferred_element_type=F32)
    return dg(a[0], b[0]) + (dg(a[0], b[1]) + dg(a[1], b[0]))


def _dot3(a, b, dims=NN):
    return _dot3s(_split2(a), _split2(b), dims)


def _dotp(a, b, passes):
    return _dot3(a, b) if passes == 3 else _dot(a, b)


GDN_PASSES = {"L2": 1, "X2": 1, "L4": 1, "X4": 1, "L8": 1, "X8": 1, "M": 1, "M2": 1, "Y": 1, "T": 1, "UW": 1}


def _split3(x):
    hi = x.astype(BF16)
    r1 = x - hi.astype(F32)
    mid = r1.astype(BF16)
    lo = (r1 - mid.astype(F32)).astype(BF16)
    return hi, mid, lo


def _dot_exact_lhs(m_bf16, x):
    hi, mid, lo = _split3(x)
    dg = functools.partial(lax.dot_general, dimension_numbers=NN, preferred_element_type=F32)
    return dg(m_bf16, hi) + (dg(m_bf16, mid) + dg(m_bf16, lo))


def _dot_exact_rhs(x, m_bf16):
    hi, mid, lo = _split3(x)
    dg = functools.partial(lax.dot_general, dimension_numbers=NN, preferred_element_type=F32)
    return dg(hi, m_bf16) + (dg(mid, m_bf16) + dg(lo, m_bf16))


def _block_tri(n, lower):
    r = lax.broadcasted_iota(jnp.int32, (n, n), 0)
    c = lax.broadcasted_iota(jnp.int32, (n, n), 1)
    same = (r // CHUNK) == (c // CHUNK)
    tri = (r >= c) if lower else (r <= c)
    return jnp.where(same & tri, 1.0, 0.0).astype(BF16)


def _ln_core(x, g, b):
    mu = jnp.mean(x, axis=-1, keepdims=True)
    xc = x - mu
    var = jnp.mean(xc * xc, axis=-1, keepdims=True)
    return xc * lax.rsqrt(var + LN_EPS) * g + b


def _ln_kernel(x_ref, g_ref, b_ref, o32_ref, o16_ref):
    y = _ln_core(x_ref[...], g_ref[...], b_ref[...])
    o32_ref[...] = y
    o16_ref[...] = y.astype(BF16)


def _ln_res_kernel(h_ref, y_ref, g_ref, b_ref, o32_ref, o16_ref):
    y = _ln_core(DEEPNORM_ALPHA * h_ref[...] + y_ref[...].astype(F32), g_ref[...], b_ref[...])
    o32_ref[...] = y
    o16_ref[...] = y.astype(BF16)


def _layer_norm(x, g, b, res=None):
    M, D = x.shape
    tm = _pick(M, (256, 128, 64, 32, 16))
    row = pl.BlockSpec((tm, D), lambda i: (i, 0))
    vec = pl.BlockSpec((1, D), lambda i: (0, 0))
    g2, b2 = g.reshape(1, D), b.reshape(1, D)
    out_shape = (jax.ShapeDtypeStruct((M, D), F32), jax.ShapeDtypeStruct((M, D), BF16))
    if res is None:
        return pl.pallas_call(_ln_kernel, grid=(M // tm,), in_specs=[row, vec, vec],
                              out_specs=(row, row), out_shape=out_shape,
                              compiler_params=_cparams(("parallel",)), name="layer_norm")(x, g2, b2)
    return pl.pallas_call(_ln_res_kernel, grid=(M // tm,), in_specs=[row, row, vec, vec],
                          out_specs=(row, row), out_shape=out_shape,
                          compiler_params=_cparams(("parallel",)), name="layer_norm_res")(res, x, g2, b2)


def _mm_kernel(a_ref, w_ref, o_ref):
    o_ref[...] = jnp.dot(a_ref[...], w_ref[...], preferred_element_type=F32).astype(o_ref.dtype)


def _mm_nt_kernel(a_ref, w_ref, o_ref):
    o_ref[...] = lax.dot_general(a_ref[...], w_ref[...], NT, preferred_element_type=F32).astype(o_ref.dtype)


def _matmul(a, w, l, out_dtype, name, w_is_nk=False, cols=None):
    M, K = a.shape
    n_all = w.shape[1] if w_is_nk else w.shape[2]
    c0, c1 = cols if cols is not None else (0, n_all)
    N = c1 - c0
    tm = _pick(M, (1024, 512, 256, 128))
    tn = _pick(math.gcd(N, c0) if c0 else N, (1024, 512, 256, 128))
    if K > 8192:
        tm, tn = min(tm, 512), min(tn, 512)
    j0 = c0 // tn
    w_spec = (pl.BlockSpec((None, tn, K), lambda i, j: (l, j0 + j, 0)) if w_is_nk
              else pl.BlockSpec((None, K, tn), lambda i, j: (l, 0, j0 + j)))
    return pl.pallas_call(
        _mm_nt_kernel if w_is_nk else _mm_kernel, grid=(M // tm, N // tn),
        in_specs=[pl.BlockSpec((tm, K), lambda i, j: (i, 0)), w_spec],
        out_specs=pl.BlockSpec((tm, tn), lambda i, j: (i, j)),
        out_shape=jax.ShapeDtypeStruct((M, N), out_dtype),
        compiler_params=_cparams(("parallel", "parallel")), name=name)(a, w)


def _glu_kernel(a_ref, wg_ref, wu_ref, o_ref):
    a = a_ref[...]
    g = jnp.dot(a, wg_ref[...].astype(BF16), preferred_element_type=F32)
    u = jnp.dot(a, wu_ref[...].astype(BF16), preferred_element_type=F32)
    o_ref[...] = (_silu(g) * u).astype(o_ref.dtype)


def _glu_matmul(a, w_gate_up, l):
    M, K = a.shape
    F = w_gate_up.shape[2] // 2
    tm = _pick(M, (1024, 512, 256, 128))
    tn = _pick(F, (512, 256, 128))
    nf = F // tn
    return pl.pallas_call(
        _glu_kernel, grid=(M // tm, nf),
        in_specs=[pl.BlockSpec((tm, K), lambda i, j: (i, 0)),
                  pl.BlockSpec((None, K, tn), lambda i, j: (l, 0, j)),
                  pl.BlockSpec((None, K, tn), lambda i, j: (l, 0, j + nf))],
        out_specs=pl.BlockSpec((tm, tn), lambda i, j: (i, j)),
        out_shape=jax.ShapeDtypeStruct((M, F), BF16),
        compiler_params=_cparams(("parallel", "parallel")), name="ffn_gate_up")(a, w_gate_up, w_gate_up)


def _merge_kernel(oa_ref, ob_ref, oc_ref, wb_ref, g0_ref, g1_ref, g2_ref, o_ref):
    gate = lambda g_ref: _sigmoid(g_ref[...].astype(F32))
    proj = lambda o_ref_b, b: jnp.dot(o_ref_b[...], wb_ref[b].astype(BF16), preferred_element_type=F32)
    acc = gate(g0_ref) * proj(oa_ref, 0)
    acc = acc + gate(g1_ref) * proj(ob_ref, 1)
    acc = acc + gate(g2_ref) * proj(oc_ref, 2)
    o_ref[...] = acc.astype(o_ref.dtype)


def _merge(o_a, o_b, o_c, w_branch, l, gate_logits, d_model):
    M = o_a.shape[0]
    tm = _pick(M, (1024, 512, 256, 128))
    tn = _pick(d_model, (512, 256, 128))
    g_step = d_model // tn
    o_spec = pl.BlockSpec((tm, BRANCH_WIDTH), lambda i, j: (i, 0))
    g_specs = [pl.BlockSpec((tm, tn), functools.partial(lambda i, j, b: (i, b * g_step + j), b=b))
               for b in range(3)]
    return pl.pallas_call(
        _merge_kernel, grid=(M // tm, d_model // tn),
        in_specs=[o_spec, o_spec, o_spec,
                  pl.BlockSpec((None, 3, BRANCH_WIDTH, tn), lambda i, j: (l, 0, 0, j))] + g_specs,
        out_specs=pl.BlockSpec((tm, tn), lambda i, j: (i, j)),
        out_shape=jax.ShapeDtypeStruct((M, d_model), BF16),
        compiler_params=_cparams(("parallel", "parallel")), name="branch_merge")(
            o_a, o_b, o_c, w_branch, gate_logits, gate_logits, gate_logits)


def _hgrn2_kernel(q_ref, f_ref, i_ref, g_ref, lb_ref, ng_ref, e_ref, o_ref, st_ref, *, tb):
    @pl.when(pl.program_id(1) == 0)
    def _():
        st_ref[...] = jnp.zeros_like(st_ref)

    lb = lb_ref[...]
    log_lb = jnp.log(lb)
    log1m_lb = jnp.log1p(-lb)
    z = f_ref[...]
    log_f = _logaddexp(log_lb, log1m_lb + _log_sigmoid(z))
    k_all = (1.0 - lb) * _sigmoid(-z)
    q_all = _silu(q_ref[...])
    v_all = i_ref[...]
    g_all = _dot_exact_lhs(_block_tri(tb, True), log_f) * LOG2_E

    row8 = lax.broadcasted_iota(jnp.int32, (SUBLANE, HEAD_DIM), 0)
    same_sub = (lax.broadcasted_iota(jnp.int32, (CHUNK, CHUNK), 0) // SUB
                == lax.broadcasted_iota(jnp.int32, (CHUNK, CHUNK), 1) // SUB)
    e_mat = e_ref[...]
    n_sub = CHUNK // SUB
    n_chunk = tb // CHUNK
    neg_inf = -jnp.inf
    zeros8 = jnp.zeros((SUBLANE, HEAD_DIM), F32)
    pairs = [(h, c) for c in range(n_chunk) for h in range(N_HEADS)]

    def part(x, h, c):
        return x[c * CHUNK:(c + 1) * CHUNK, h * HEAD_DIM:(h + 1) * HEAD_DIM]

    G = {p: part(g_all, *p) for p in pairs}
    q = {p: part(q_all, *p) for p in pairs}
    k = {p: part(k_all, *p) for p in pairs}
    v = {p: part(v_all, *p) for p in pairs}

    def diag_products(G, q, k):
        pcs = []
        for I in range(n_sub):
            s0 = I * SUB
            g_top, g_bot = G[s0:s0 + SUBLANE], G[s0 + SUBLANE:s0 + SUB]
            q_top, q_bot = q[s0:s0 + SUBLANE], q[s0 + SUBLANE:s0 + SUB]
            cols = []
            for j in range(SUB):
                r = s0 + j
                g_r, k_r = G[r:r + 1], k[r:r + 1]
                if j < SUBLANE:
                    top = q_top * jnp.exp2(jnp.where(row8 >= j, g_top - g_r, neg_inf)) * k_r
                    bot = q_bot * jnp.exp2(g_bot - g_r) * k_r
                else:
                    top = zeros8
                    bot = q_bot * jnp.exp2(jnp.where(row8 >= j - SUBLANE, g_bot - g_r, neg_inf)) * k_r
                cols.append(jnp.concatenate([top, bot], axis=0).astype(BF16))
            pcs.append(jnp.concatenate(cols, axis=1))
        return jnp.concatenate(pcs, axis=0)

    pcat = {p: diag_products(G[p], q[p], k[p]) for p in pairs}
    dfull = {p: jnp.where(same_sub, jnp.dot(pcat[p], e_mat, preferred_element_type=F32), 0.0) for p in pairs}

    def below_diag(G, q, k, dfull):
        a_rows = [dfull[0:SUB]]
        for I in range(1, n_sub):
            s0 = I * SUB
            gb = G[s0 - 1:s0]
            qt = q[s0:s0 + SUB] * jnp.exp2(G[s0:s0 + SUB] - gb)
            kx = jnp.concatenate([k[:s0] * jnp.exp2(gb - G[:s0]), jnp.zeros((CHUNK - s0, HEAD_DIM), F32)], axis=0)
            a_rows.append(_dot(qt, kx, NT) + dfull[s0:s0 + SUB])
        return jnp.concatenate(a_rows, axis=0)

    a_mat = {p: below_diag(G[p], q[p], k[p], dfull[p]) for p in pairs}
    o_intra = {p: _dot(a_mat[p], v[p]) for p in pairs}
    qe = {p: q[p] * jnp.exp2(G[p]) for p in pairs}
    g_last = {p: G[p][CHUNK - 1:CHUNK] for p in pairs}
    kd = {p: k[p] * jnp.exp2(g_last[p] - G[p]) for p in pairs}

    st = [st_ref[h] for h in range(N_HEADS)]
    o_rows = []
    for c in range(n_chunk):
        o_rows.append([o_intra[h, c] + _dot(qe[h, c], st[h], NT) for h in range(N_HEADS)])
        st = [st[h] * jnp.exp2(g_last[h, c]) + _dot(v[h, c], kd[h, c], TN) for h in range(N_HEADS)]
    for h in range(N_HEADS):
        st_ref[h] = st[h]
    ng = ng_ref[...]
    gate = _silu(g_ref[...])
    outs = []
    for h in range(N_HEADS):
        o = jnp.concatenate([o_rows[c][h] for c in range(n_chunk)], axis=0) if n_chunk > 1 else o_rows[0][h]
        o = o * lax.rsqrt(jnp.mean(o * o, axis=-1, keepdims=True) + RMS_EPS) * ng
        outs.append((o * gate[:, h * HEAD_DIM:(h + 1) * HEAD_DIM]).astype(o_ref.dtype))
    o_ref[...] = jnp.concatenate(outs, axis=1)


def _hgrn2_emat():
    e = np.zeros((SUB * HEAD_DIM, CHUNK), np.float32)
    for j in range(SUB):
        for I in range(CHUNK // SUB):
            e[j * HEAD_DIM:(j + 1) * HEAD_DIM, I * SUB + j] = 1.0
    return jnp.asarray(e, BF16)


def _hgrn2(proj, lb, norm_g, B, T):
    tb = _pick(T, (128, 64))
    nt = T // tb
    cb = lambda col: pl.BlockSpec((tb, BRANCH_WIDTH), lambda b, t, col=col: (b * nt + t, col // BRANCH_WIDTH))
    const = lambda shape: pl.BlockSpec(shape, lambda b, t: (0,) * len(shape))
    return pl.pallas_call(
        functools.partial(_hgrn2_kernel, tb=tb), grid=(B, nt),
        in_specs=[cb(COL_HG_Q), cb(COL_HG_F), cb(COL_HG_I), cb(COL_HG_G),
                  const((1, BRANCH_WIDTH)), const((1, HEAD_DIM)), const((SUB * HEAD_DIM, CHUNK))],
        out_specs=pl.BlockSpec((tb, BRANCH_WIDTH), lambda b, t: (b * nt + t, 0)),
        out_shape=jax.ShapeDtypeStruct((B * T, BRANCH_WIDTH), BF16),
        scratch_shapes=[pltpu.VMEM((N_HEADS, HEAD_DIM, HEAD_DIM), F32)],
        compiler_params=_cparams(("parallel", "arbitrary")), name="hgrn2")(
            proj, proj, proj, proj, lb.reshape(1, BRANCH_WIDTH), norm_g.reshape(1, HEAD_DIM), _hgrn2_emat())


def _l2n(x):
    return x * lax.rsqrt(jnp.sum(x * x, axis=-1, keepdims=True) + L2_EPS)


def _gdn_kernel(qkv_ref, prev_ref, z_ref, ab_ref, abt_ref, w_ref, hrow_ref, hcol_ref, ng_ref,
                o_ref, s_ref, *, tb):
    first = pl.program_id(1) == 0

    @pl.when(first)
    def _():
        s_ref[...] = jnp.zeros_like(s_ref)

    x = qkv_ref[...]
    prev = jnp.where(first, 0.0, prev_ref[...])
    xf = jnp.concatenate([prev, x], axis=0)
    w = w_ref[...]
    acc = x * w[GDN_CONV - 1:GDN_CONV]
    for j in range(GDN_CONV - 1):
        off = SUBLANE - (GDN_CONV - 1) + j
        acc = acc + xf[off:off + tb] * w[j:j + 1]
    y = _silu(acc)

    hrow = hrow_ref[...]
    hcol = hcol_ref[...]
    ab = ab_ref[...]
    g_cols = -jnp.exp(hrow[0:1]) * _softplus(ab + hrow[1:2])
    gc_cols = _dot_exact_lhs(_block_tri(tb, True), g_cols)
    beta_cols = _sigmoid(ab)
    abt = abt_ref[...]
    g_rows = -jnp.exp(hcol[:, 0:1]) * _softplus(abt[0:N_HEADS] + hcol[:, LANE:LANE + 1])
    gc_rows = _dot_exact_rhs(g_rows, _block_tri(tb, False))

    ri = lax.broadcasted_iota(jnp.int32, (CHUNK, CHUNK), 0)
    ci = lax.broadcasted_iota(jnp.int32, (CHUNK, CHUNK), 1)
    causal = ri >= ci
    strict = ri > ci
    same_sub = (ri // SUB) == (ci // SUB)
    eye = jnp.where(ri == ci, 1.0, 0.0)
    neg_inf = -jnp.inf
    n_chunk = tb // CHUNK
    pairs = [(h, c) for c in range(n_chunk) for h in range(N_HEADS)]

    def head_cols(base, h):
        return y[:, base + h * HEAD_DIM: base + (h + 1) * HEAD_DIM]

    q_h = [_l2n(head_cols(0, h)) * (HEAD_DIM ** -0.5) for h in range(N_HEADS)]
    k_h = [_l2n(head_cols(BRANCH_WIDTH, h)) for h in range(N_HEADS)]
    v_h = [head_cols(2 * BRANCH_WIDTH, h) for h in range(N_HEADS)]

    rows = lambda c: slice(c * CHUNK, (c + 1) * CHUNK)
    q = {(h, c): q_h[h][rows(c)] for h, c in pairs}
    k = {(h, c): k_h[h][rows(c)] for h, c in pairs}
    v = {(h, c): v_h[h][rows(c)] for h, c in pairs}
    gc = {(h, c): gc_cols[rows(c), h:h + 1] for h, c in pairs}
    bt = {(h, c): beta_cols[rows(c), N_HEADS + h:N_HEADS + h + 1] for h, c in pairs}
    decay = {(h, c): jnp.exp(jnp.where(causal, gc[h, c] - gc_rows[h:h + 1, rows(c)], neg_inf)) for h, c in pairs}
    kb = {p: k[p] * bt[p] for p in pairs}
    L = {p: jnp.where(strict, _dot(kb[p], k[p], NT) * decay[p], 0.0) for p in pairs}
    qk = {p: jnp.where(causal, _dot(q[p], k[p], NT) * decay[p], 0.0) for p in pairs}
    Ld = {p: jnp.where(same_sub, L[p], 0.0) for p in pairs}
    Lo = {p: L[p] - Ld[p] for p in pairs}
    X = {p: eye - Ld[p] for p in pairs}
    P = {p: _dotp(Ld[p], Ld[p], GDN_PASSES["L2"]) for p in pairs}
    for it, (px, pp) in enumerate((("X2", "L4"), ("X4", "L8"), ("X8", None))):
        X = {p: X[p] + _dotp(X[p], P[p], GDN_PASSES[px]) for p in pairs}
        if pp is not None:
            P = {p: _dotp(P[p], P[p], GDN_PASSES[pp]) for p in pairs}
    M = {p: _dotp(X[p], Lo[p], GDN_PASSES["M"]) for p in pairs}
    M2 = {p: _dotp(M[p], M[p], GDN_PASSES["M2"]) for p in pairs}
    Y = {p: (eye - M[p]) + _dotp(eye - M[p], M2[p], GDN_PASSES["Y"]) for p in pairs}
    Tm = {p: _dotp(Y[p], X[p], GDN_PASSES["T"]) for p in pairs}
    uw = {p: _dotp(Tm[p], jnp.concatenate([v[p] * bt[p], kb[p] * jnp.exp(gc[p])], axis=1), GDN_PASSES["UW"])
          for p in pairs}
    qe = {p: q[p] * jnp.exp(gc[p]) for p in pairs}
    g_last = {p: gc[p][CHUNK - 1:CHUNK] for p in pairs}
    kd = {p: k[p] * jnp.exp(g_last[p] - gc[p]) for p in pairs}

    S = [s_ref[h] for h in range(N_HEADS)]
    o_rows = []
    for c in range(n_chunk):
        v_new = [uw[h, c][:, :HEAD_DIM] - _dot(uw[h, c][:, HEAD_DIM:], S[h]) for h in range(N_HEADS)]
        o_c = [_dot(qe[h, c], S[h]) + _dot(qk[h, c], v_new[h]) for h in range(N_HEADS)]
        S = [S[h] * jnp.exp(g_last[h, c]) + _dot(kd[h, c], v_new[h], TN) for h in range(N_HEADS)]
        o_rows.append(o_c)
    for h in range(N_HEADS):
        s_ref[h] = S[h]
    ng = ng_ref[...]
    z = z_ref[...]
    outs = []
    for h in range(N_HEADS):
        o = jnp.concatenate([o_rows[c][h] for c in range(n_chunk)], axis=0) if n_chunk > 1 else o_rows[0][h]
        o = o * lax.rsqrt(jnp.mean(o * o, axis=-1, keepdims=True) + RMS_EPS) * ng
        outs.append((o * _silu(z[:, h * HEAD_DIM:(h + 1) * HEAD_DIM])).astype(o_ref.dtype))
    o_ref[...] = jnp.concatenate(outs, axis=1)


def _gdn(proj, ab_t, conv_w, a_log, dt_bias, norm_g, B, T):
    tb = _pick(T, (128, 64))
    nt = T // tb
    r8 = tb // SUBLANE
    w3 = 3 * BRANCH_WIDTH
    pad = lambda p: jnp.pad(p.astype(F32), (0, LANE - N_HEADS))
    hrow = jnp.stack([pad(a_log), pad(dt_bias)])
    bc = lambda p: jnp.broadcast_to(p.astype(F32)[:, None], (N_HEADS, LANE))
    hcol = jnp.concatenate([bc(a_log), bc(dt_bias)], axis=1)
    const = lambda shape: pl.BlockSpec(shape, lambda b, t: (0,) * len(shape))
    return pl.pallas_call(
        functools.partial(_gdn_kernel, tb=tb), grid=(B, nt),
        in_specs=[pl.BlockSpec((tb, w3), lambda b, t: (b * nt + t, COL_GDN_QKV // w3)),
                  pl.BlockSpec((SUBLANE, w3),
                               lambda b, t: (jnp.maximum((b * nt + t) * r8 - 1, 0), COL_GDN_QKV // w3)),
                  pl.BlockSpec((tb, BRANCH_WIDTH), lambda b, t: (b * nt + t, COL_GDN_Z // BRANCH_WIDTH)),
                  pl.BlockSpec((tb, LANE), lambda b, t: (b * nt + t, COL_AB // LANE)),
                  pl.BlockSpec((2 * N_HEADS, tb), lambda b, t: (0, b * nt + t)),
                  const((GDN_CONV, w3)), const((2, LANE)), const((N_HEADS, 2 * LANE)), const((1, HEAD_DIM))],
        out_specs=pl.BlockSpec((tb, BRANCH_WIDTH), lambda b, t: (b * nt + t, 0)),
        out_shape=jax.ShapeDtypeStruct((B * T, BRANCH_WIDTH), BF16),
        scratch_shapes=[pltpu.VMEM((N_HEADS, HEAD_DIM, HEAD_DIM), F32)],
        compiler_params=_cparams(("parallel", "arbitrary")), name="gated_deltanet")(
            proj, proj, proj, proj, ab_t, conv_w, hrow, hcol, norm_g.reshape(1, HEAD_DIM))


def _swa_kernel(q_ref, kc_ref, vc_ref, kp_ref, vp_ref, sink_ref, o_ref):
    has_prev = pl.program_id(1) > 0
    qi = lax.broadcasted_iota(jnp.int32, (SWA_BLOCK, SWA_BLOCK), 0)
    kj = lax.broadcasted_iota(jnp.int32, (SWA_BLOCK, SWA_BLOCK), 1)
    dist_c = (qi - kj).astype(F32)
    dist_p = dist_c + float(SWA_BLOCK)
    valid_c = qi >= kj
    valid_p = (kj > qi) & has_prev
    neg_inf = -jnp.inf
    group = SWA_Q_HEADS // SWA_KV_HEADS
    q_all = q_ref[...] * (SWA_HEAD_DIM ** -0.5)
    kc, vc, kp, vp = kc_ref[...], vc_ref[...], kp_ref[...], vp_ref[...]
    sinks = sink_ref[...]
    heads = range(SWA_Q_HEADS)
    ks = [slice((hq // group) * SWA_HEAD_DIM, (hq // group + 1) * SWA_HEAD_DIM) for hq in heads]
    slope = [2.0 ** (-8.0 * (hq + 1) / SWA_Q_HEADS) for hq in heads]
    q = [q_all[:, hq * SWA_HEAD_DIM:(hq + 1) * SWA_HEAD_DIM].astype(BF16) for hq in heads]
    kcb, kpb, vcb, vpb = kc.astype(BF16), kp.astype(BF16), vc.astype(BF16), vp.astype(BF16)
    s_c = [jnp.where(valid_c, _dot(q[h], kcb[:, ks[h]], NT) - slope[h] * dist_c, neg_inf) for h in heads]
    s_p = [jnp.where(valid_p, _dot(q[h], kpb[:, ks[h]], NT) - slope[h] * dist_p, neg_inf) for h in heads]
    sink = [sinks[h:h + 1, :1] for h in heads]
    m = [jnp.maximum(jnp.max(jnp.maximum(s_c[h], s_p[h]), axis=-1, keepdims=True), sink[h]) for h in heads]
    p_c = [jnp.exp(s_c[h] - m[h]) for h in heads]
    p_p = [jnp.exp(s_p[h] - m[h]) for h in heads]
    inv = [1.0 / (jnp.sum(p_c[h] + p_p[h], axis=-1, keepdims=True) + jnp.exp(sink[h] - m[h])) for h in heads]
    outs = [_dot(p_c[h] * inv[h], vcb[:, ks[h]]) + _dot(p_p[h] * inv[h], vpb[:, ks[h]]) for h in heads]
    o_ref[...] = jnp.concatenate(outs, axis=1).astype(o_ref.dtype)


def _swa(proj, sinks, B, T):
    nb = T // SWA_BLOCK
    qw = SWA_Q_HEADS * SWA_HEAD_DIM
    cur = lambda col: pl.BlockSpec((SWA_BLOCK, LANE), lambda b, n, col=col: (b * nb + n, col))
    prv = lambda col: pl.BlockSpec((SWA_BLOCK, LANE),
                                   lambda b, n, col=col: (b * nb + jnp.maximum(n - 1, 0), col))
    sink_b = jnp.broadcast_to(sinks.astype(F32)[:, None], (SWA_Q_HEADS, LANE))
    return pl.pallas_call(
        _swa_kernel, grid=(B, nb),
        in_specs=[pl.BlockSpec((SWA_BLOCK, qw), lambda b, n: (b * nb + n, COL_SWA_Q // qw)),
                  cur(COL_SWA_K // LANE), cur(COL_SWA_V // LANE), prv(COL_SWA_K // LANE), prv(COL_SWA_V // LANE),
                  pl.BlockSpec((SWA_Q_HEADS, LANE), lambda b, n: (0, 0))],
        out_specs=pl.BlockSpec((SWA_BLOCK, qw), lambda b, n: (b * nb + n, 0)),
        out_shape=jax.ShapeDtypeStruct((B * T, qw), BF16),
        compiler_params=_cparams(("parallel", "arbitrary")), name="swa")(
            proj, proj, proj, proj, proj, sink_b)


SRC_HG, SRC_QKV, SRC_Z, SRC_AB = 0, 4096, 7168, 8192
N_AB = 2 * N_HEADS
SWA_COLS = COL_AB - COL_SWA_Q


def _w_in_kernel(w_ref, o_ref, *, tk):
    def put(dst, src, n):
        o_ref[dst:dst + n, :] = w_ref[src:src + n, :].astype(BF16)

    put(COL_GDN_QKV, SRC_QKV, 3 * BRANCH_WIDTH)
    put(COL_HG_Q, SRC_HG, 4 * BRANCH_WIDTH)
    put(COL_GDN_Z, SRC_Z, BRANCH_WIDTH)
    put(COL_SWA_Q, SRC_AB + N_AB, SWA_COLS)
    put(COL_AB, SRC_AB, N_AB)
    o_ref[COL_AB + N_AB:COL_GATES, :] = jnp.zeros((COL_GATES - COL_AB - N_AB, tk), BF16)
    put(COL_GATES, SRC_AB + N_AB + SWA_COLS, o_ref.shape[0] - COL_GATES)


def _prep_w_in(w):
    L, d, n_src = w.shape
    n_out = COL_GATES + (n_src - SRC_AB - N_AB - SWA_COLS)
    tk = LANE
    return pl.pallas_call(
        functools.partial(_w_in_kernel, tk=tk), grid=(L, d // tk),
        in_specs=[pl.BlockSpec((None, n_src, tk), lambda l, i: (l, 0, i))],
        out_specs=pl.BlockSpec((None, n_out, tk), lambda l, i: (l, 0, i)),
        out_shape=jax.ShapeDtypeStruct((L, n_out, d), BF16),
        compiler_params=_cparams(("parallel", "parallel")), name="w_in_relayout")(jnp.swapaxes(w, 1, 2))


def kernel(x, ln_in_g, ln_in_b, hg_lb_logits, w_in, gdn_conv_w, gdn_a_log, gdn_dt_bias, hg_norm_g,
           gdn_norm_g, swa_sinks, w_branch, w_out, ln1_g, ln1_b, w_gate_up, w_down, ln2_g, ln2_b):
    B, T, D = x.shape
    M = B * T
    depth = w_in.shape[0]
    lb_all = jnp.cumsum(jax.nn.softmax(hg_lb_logits.astype(F32), axis=0), axis=0)
    lb_all = lb_all - lb_all[0]

    w_in16 = _prep_w_in(w_in)
    w_out16 = w_out.astype(BF16)
    w_down16 = w_down.astype(BF16)

    h32, h16 = _layer_norm(x.reshape(M, D), ln_in_g, ln_in_b)
    for l in range(depth):
        proj = _matmul(h16, w_in16, l, F32, "in_proj", w_is_nk=True, cols=(0, COL_GATES))
        gate_logits = _matmul(h16, w_in16, l, BF16, "in_proj_gates", w_is_nk=True,
                              cols=(COL_GATES, w_in16.shape[1]))
        ab_t = proj[:, COL_AB:COL_AB + N_AB].T
        o_a = _hgrn2(proj, lb_all[l], hg_norm_g[l], B, T)
        o_b = _gdn(proj, ab_t, gdn_conv_w[l], gdn_a_log[l], gdn_dt_bias[l], gdn_norm_g[l], B, T)
        o_c = _swa(proj, swa_sinks[l], B, T)
        merged = _merge(o_a, o_b, o_c, w_branch, l, gate_logits, D)
        mix = _matmul(merged, w_out16, l, BF16, "out_proj")
        h32, h16 = _layer_norm(mix, ln1_g[l], ln1_b[l], res=h32)
        ff = _glu_matmul(h16, w_gate_up, l)
        ff = _matmul(ff, w_down16, l, BF16, "ffn_down")
        h32, h16 = _layer_norm(ff, ln2_g[l], ln2_b[l], res=h32)
    return h32.reshape(B, T, D)
```

```python
import functools
import math

import jax
import jax.numpy as jnp
import numpy as np
from jax import lax
from jax.experimental import pallas as pl
from jax.experimental.pallas import tpu as pltpu

F32 = jnp.float32
BF16 = jnp.bfloat16

N_HEADS = 8
HEAD_DIM = 128
BRANCH_WIDTH = N_HEADS * HEAD_DIM
GDN_CONV = 4
SWA_Q_HEADS = 16
SWA_KV_HEADS = 2
SWA_HEAD_DIM = 64
SWA_BLOCK = 128
CHUNK = 64
SUB = 16
DEPTH = 2
DEEPNORM_ALPHA = (2 * DEPTH) ** 0.25
LN_EPS = 1e-5
RMS_EPS = 1e-6
L2_EPS = 1e-6
LOG2_E = math.log2(math.e)

COL_GDN_QKV = 0
COL_HG_Q, COL_HG_F, COL_HG_I, COL_HG_G = 3072, 4096, 5120, 6144
COL_GDN_Z = 7168
COL_SWA_Q, COL_SWA_K, COL_SWA_V = 8192, 9216, 9344
COL_AB = 9472
COL_GATES = 10240
LANE = 128
SUBLANE = 8

VMEM_LIMIT = 56 * 1024 * 1024
VMEM_LIMIT_HIGH = 61 * 1024 * 1024


def _cparams(sem, vmem=VMEM_LIMIT):
    return pltpu.CompilerParams(dimension_semantics=sem, vmem_limit_bytes=vmem)


def _pick(n, cands):
    for c in cands:
        if n % c == 0:
            return c
    raise ValueError(f"no tile for {n} in {cands}")


def _sigmoid(x):
    return 1.0 / (1.0 + jnp.exp(-x))


def _silu(x):
    return x * _sigmoid(x)


def _log_sigmoid(x):
    return jnp.minimum(x, 0.0) - jnp.log1p(jnp.exp(-jnp.abs(x)))


def _softplus(x):
    return jnp.maximum(x, 0.0) + jnp.log1p(jnp.exp(-jnp.abs(x)))


def _logaddexp(a, b):
    return jnp.maximum(a, b) + jnp.log1p(jnp.exp(-jnp.abs(a - b)))


NN = (((1,), (0,)), ((), ()))
NT = (((1,), (1,)), ((), ()))
TN = (((0,), (0,)), ((), ()))


def _dot(a, b, dims=NN):
    return lax.dot_general(a.astype(BF16), b.astype(BF16), dims, preferred_element_type=F32)


def _split2(x):
    hi = x.astype(BF16)
    lo = (x - hi.astype(F32)).astype(BF16)
    return hi, lo


def _dot3s(a, b, dims=NN):
    dg = functools.partial(lax.dot_general, dimension_numbers=dims, preferred_element_type=F32)
    return dg(a[0], b[0]) + (dg(a[0], b[1]) + dg(a[1], b[0]))


def _dot3(a, b, dims=NN):
    return _dot3s(_split2(a), _split2(b), dims)


def _dotp(a, b, passes):
    return _dot3(a, b) if passes == 3 else _dot(a, b)


GDN_PASSES = {"L2": 1, "X2": 1, "L4": 1, "X4": 1, "L8": 1, "X8": 1, "M": 1, "M2": 1, "Y": 1, "T": 1, "UW": 1}


def _split3(x):
    hi = x.astype(BF16)
    r1 = x - hi.astype(F32)
    mid = r1.astype(BF16)
    lo = (r1 - mid.astype(F32)).astype(BF16)
    return hi, mid, lo


def _dot_exact_lhs(m_bf16, x):
    hi, mid, lo = _split3(x)
    dg = functools.partial(lax.dot_general, dimension_numbers=NN, preferred_element_type=F32)
    return dg(m_bf16, hi) + (dg(m_bf16, mid) + dg(m_bf16, lo))


def _dot_exact_rhs(x, m_bf16):
    hi, mid, lo = _split3(x)
    dg = functools.partial(lax.dot_general, dimension_numbers=NN, preferred_element_type=F32)
    return dg(hi, m_bf16) + (dg(mid, m_bf16) + dg(lo, m_bf16))


def _block_tri(n, lower):
    r = lax.broadcasted_iota(jnp.int32, (n, n), 0)
    c = lax.broadcasted_iota(jnp.int32, (n, n), 1)
    same = (r // CHUNK) == (c // CHUNK)
    tri = (r >= c) if lower else (r <= c)
    return jnp.where(same & tri, 1.0, 0.0).astype(BF16)


def _ln_core(x, g, b):
    mu = jnp.mean(x, axis=-1, keepdims=True)
    xc = x - mu
    var = jnp.mean(xc * xc, axis=-1, keepdims=True)
    return xc * lax.rsqrt(var + LN_EPS) * g + b


def _ln_kernel(x_ref, g_ref, b_ref, o32_ref, o16_ref):
    y = _ln_core(x_ref[...], g_ref[...], b_ref[...])
    o32_ref[...] = y
    o16_ref[...] = y.astype(BF16)


def _ln_res_kernel(h_ref, y_ref, g_ref, b_ref, o32_ref, o16_ref):
    y = _ln_core(DEEPNORM_ALPHA * h_ref[...] + y_ref[...].astype(F32), g_ref[...], b_ref[...])
    o32_ref[...] = y
    o16_ref[...] = y.astype(BF16)


def _layer_norm(x, g, b, res=None):
    M, D = x.shape
    tm = _pick(M, (256, 128, 64, 32, 16))
    row = pl.BlockSpec((tm, D), lambda i: (i, 0))
    vec = pl.BlockSpec((1, D), lambda i: (0, 0))
    g2, b2 = g.reshape(1, D), b.reshape(1, D)
    out_shape = (jax.ShapeDtypeStruct((M, D), F32), jax.ShapeDtypeStruct((M, D), BF16))
    if res is None:
        return pl.pallas_call(_ln_kernel, grid=(M // tm,), in_specs=[row, vec, vec],
                              out_specs=(row, row), out_shape=out_shape,
                              compiler_params=_cparams(("parallel",)), name="layer_norm")(x, g2, b2)
    return pl.pallas_call(_ln_res_kernel, grid=(M // tm,), in_specs=[row, row, vec, vec],
                          out_specs=(row, row), out_shape=out_shape,
                          compiler_params=_cparams(("parallel",)), name="layer_norm_res")(res, x, g2, b2)


def _mm_kernel(a_ref, w_ref, o_ref):
    o_ref[...] = jnp.dot(a_ref[...], w_ref[...], preferred_element_type=F32).astype(o_ref.dtype)


def _mm_nt_kernel(a_ref, w_ref, o_ref):
    o_ref[...] = lax.dot_general(a_ref[...], w_ref[...], NT, preferred_element_type=F32).astype(o_ref.dtype)


def _matmul(a, w, l, out_dtype, name, w_is_nk=False, cols=None):
    M, K = a.shape
    n_all = w.shape[1] if w_is_nk else w.shape[2]
    c0, c1 = cols if cols is not None else (0, n_all)
    N = c1 - c0
    tm = _pick(M, (1024, 512, 256, 128))
    tn = _pick(math.gcd(N, c0) if c0 else N, (1024, 512, 256, 128))
    if K > 8192:
        tm, tn = min(tm, 512), min(tn, 512)
    j0 = c0 // tn
    w_spec = (pl.BlockSpec((None, tn, K), lambda i, j: (l, j0 + j, 0)) if w_is_nk
              else pl.BlockSpec((None, K, tn), lambda i, j: (l, 0, j0 + j)))
    return pl.pallas_call(
        _mm_nt_kernel if w_is_nk else _mm_kernel, grid=(M // tm, N // tn),
        in_specs=[pl.BlockSpec((tm, K), lambda i, j: (i, 0)), w_spec],
        out_specs=pl.BlockSpec((tm, tn), lambda i, j: (i, j)),
        out_shape=jax.ShapeDtypeStruct((M, N), out_dtype),
        compiler_params=_cparams(("parallel", "parallel")), name=name)(a, w)


def _mm_ln_kernel(a_ref, w_ref, res_ref, g_ref, b_ref, o32_ref, o16_ref, x_scr, s1_scr, s2_scr, *, nj, tn):
    j = pl.program_id(1)
    x = DEEPNORM_ALPHA * res_ref[...] + jnp.dot(a_ref[...], w_ref[...], preferred_element_type=F32)
    x_scr[j] = x
    p1 = x[:, :LANE]
    p2 = p1 * p1
    for c in range(1, tn // LANE):
        xc = x[:, c * LANE:(c + 1) * LANE]
        p1 = p1 + xc
        p2 = p2 + xc * xc

    @pl.when(j == 0)
    def _():
        s1_scr[...] = p1
        s2_scr[...] = p2

    @pl.when(j > 0)
    def _():
        s1_scr[...] += p1
        s2_scr[...] += p2

    @pl.when(j == nj - 1)
    def _():
        d = nj * tn
        mu = jnp.sum(s1_scr[...], axis=-1, keepdims=True) * (1.0 / d)
        var = jnp.maximum(jnp.sum(s2_scr[...], axis=-1, keepdims=True) * (1.0 / d) - mu * mu, 0.0)
        rstd = lax.rsqrt(var + LN_EPS)
        for jj in range(nj):
            cols = slice(jj * tn, (jj + 1) * tn)
            y = (x_scr[jj] - mu) * rstd * g_ref[:, cols] + b_ref[:, cols]
            o32_ref[:, cols] = y
            o16_ref[:, cols] = y.astype(BF16)


def _matmul_ln(a, w, l, res, g, b, name):
    M, K = a.shape
    D = w.shape[2]
    tm = _pick(M, (512, 256, 128))
    tn = _pick(D, (512, 256, 128))
    nj = D // tn
    row = pl.BlockSpec((tm, D), lambda i, j: (i, 0))
    vec = pl.BlockSpec((1, D), lambda i, j: (0, 0))
    return pl.pallas_call(
        functools.partial(_mm_ln_kernel, nj=nj, tn=tn), grid=(M // tm, nj),
        in_specs=[pl.BlockSpec((tm, K), lambda i, j: (i, 0)), pl.BlockSpec((None, K, tn), lambda i, j: (l, 0, j)),
                  pl.BlockSpec((tm, tn), lambda i, j: (i, j)), vec, vec],
        out_specs=(row, row),
        out_shape=(jax.ShapeDtypeStruct((M, D), F32), jax.ShapeDtypeStruct((M, D), BF16)),
        scratch_shapes=[pltpu.VMEM((nj, tm, tn), F32), pltpu.VMEM((tm, LANE), F32), pltpu.VMEM((tm, LANE), F32)],
        compiler_params=_cparams(("parallel", "arbitrary"), VMEM_LIMIT_HIGH), name=name)(
            a, w, res, g.reshape(1, D), b.reshape(1, D))


def _glu_kernel(a_ref, wg_ref, wu_ref, o_ref):
    a = a_ref[...]
    g = jnp.dot(a, wg_ref[...].astype(BF16), preferred_element_type=F32)
    u = jnp.dot(a, wu_ref[...].astype(BF16), preferred_element_type=F32)
    o_ref[...] = (_silu(g) * u).astype(o_ref.dtype)


def _glu_matmul(a, w_gate_up, l):
    M, K = a.shape
    F = w_gate_up.shape[2] // 2
    tm = _pick(M, (1024, 512, 256, 128))
    tn = _pick(F, (512, 256, 128))
    nf = F // tn
    return pl.pallas_call(
        _glu_kernel, grid=(M // tm, nf),
        in_specs=[pl.BlockSpec((tm, K), lambda i, j: (i, 0)),
                  pl.BlockSpec((None, K, tn), lambda i, j: (l, 0, j)),
                  pl.BlockSpec((None, K, tn), lambda i, j: (l, 0, j + nf))],
        out_specs=pl.BlockSpec((tm, tn), lambda i, j: (i, j)),
        out_shape=jax.ShapeDtypeStruct((M, F), BF16),
        compiler_params=_cparams(("parallel", "parallel")), name="ffn_gate_up")(a, w_gate_up, w_gate_up)


def _merge_kernel(oa_ref, ob_ref, oc_ref, wb_ref, g0_ref, g1_ref, g2_ref, o_ref):
    gate = lambda g_ref: _sigmoid(g_ref[...].astype(F32))
    proj = lambda o_ref_b, b: jnp.dot(o_ref_b[...], wb_ref[b], preferred_element_type=F32)
    acc = gate(g0_ref) * proj(oa_ref, 0)
    acc = acc + gate(g1_ref) * proj(ob_ref, 1)
    acc = acc + gate(g2_ref) * proj(oc_ref, 2)
    o_ref[...] = acc.astype(o_ref.dtype)


def _merge(o_a, o_b, o_c, w_branch, l, gate_logits, d_model):
    M = o_a.shape[0]
    tm = _pick(M, (1024, 512, 256, 128))
    tn = _pick(d_model, (512, 256, 128))
    g_step = d_model // tn
    o_spec = pl.BlockSpec((tm, BRANCH_WIDTH), lambda i, j: (i, 0))
    g_specs = [pl.BlockSpec((tm, tn), functools.partial(lambda i, j, b: (i, b * g_step + j), b=b))
               for b in range(3)]
    return pl.pallas_call(
        _merge_kernel, grid=(M // tm, d_model // tn),
        in_specs=[o_spec, o_spec, o_spec,
                  pl.BlockSpec((None, 3, BRANCH_WIDTH, tn), lambda i, j: (l, 0, 0, j))] + g_specs,
        out_specs=pl.BlockSpec((tm, tn), lambda i, j: (i, j)),
        out_shape=jax.ShapeDtypeStruct((M, d_model), BF16),
        compiler_params=_cparams(("parallel", "parallel")), name="branch_merge")(
            o_a, o_b, o_c, w_branch, gate_logits, gate_logits, gate_logits)


def _hgrn2_kernel(q_ref, f_ref, i_ref, g_ref, lb_ref, ng_ref, e_ref, o_ref, st_ref, *, tb):
    @pl.when(pl.program_id(1) == 0)
    def _():
        st_ref[...] = jnp.zeros_like(st_ref)

    lb = lb_ref[...]
    log_lb = jnp.log(lb)
    log1m_lb = jnp.log1p(-lb)
    z = f_ref[...]
    log_f = _logaddexp(log_lb, log1m_lb + _log_sigmoid(z))
    k_all = (1.0 - lb) * _sigmoid(-z)
    q_all = _silu(q_ref[...])
    v_all = i_ref[...]
    g_all = _dot_exact_lhs(_block_tri(tb, True), log_f) * LOG2_E

    row8 = lax.broadcasted_iota(jnp.int32, (SUBLANE, HEAD_DIM), 0)
    same_sub = (lax.broadcasted_iota(jnp.int32, (CHUNK, CHUNK), 0) // SUB
                == lax.broadcasted_iota(jnp.int32, (CHUNK, CHUNK), 1) // SUB)
    e_mat = e_ref[...]
    n_sub = CHUNK // SUB
    n_chunk = tb // CHUNK
    neg_inf = -jnp.inf
    zeros8 = jnp.zeros((SUBLANE, HEAD_DIM), F32)
    pairs = [(h, c) for c in range(n_chunk) for h in range(N_HEADS)]

    def part(x, h, c):
        return x[c * CHUNK:(c + 1) * CHUNK, h * HEAD_DIM:(h + 1) * HEAD_DIM]

    G = {p: part(g_all, *p) for p in pairs}
    q = {p: part(q_all, *p) for p in pairs}
    k = {p: part(k_all, *p) for p in pairs}
    v = {p: part(v_all, *p) for p in pairs}

    def diag_products(G, q, k):
        pcs = []
        for I in range(n_sub):
            s0 = I * SUB
            g_top, g_bot = G[s0:s0 + SUBLANE], G[s0 + SUBLANE:s0 + SUB]
            q_top, q_bot = q[s0:s0 + SUBLANE], q[s0 + SUBLANE:s0 + SUB]
            cols = []
            for j in range(SUB):
                r = s0 + j
                g_r, k_r = G[r:r + 1], k[r:r + 1]
                if j < SUBLANE:
                    top = q_top * jnp.exp2(jnp.where(row8 >= j, g_top - g_r, neg_inf)) * k_r
                    bot = q_bot * jnp.exp2(g_bot - g_r) * k_r
                else:
                    top = zeros8
                    bot = q_bot * jnp.exp2(jnp.where(row8 >= j - SUBLANE, g_bot - g_r, neg_inf)) * k_r
                cols.append(jnp.concatenate([top, bot], axis=0).astype(BF16))
            pcs.append(jnp.concatenate(cols, axis=1))
        return jnp.concatenate(pcs, axis=0)

    pcat = {p: diag_products(G[p], q[p], k[p]) for p in pairs}
    dfull = {p: jnp.where(same_sub, jnp.dot(pcat[p], e_mat, preferred_element_type=F32), 0.0) for p in pairs}

    def below_diag(G, q, k, dfull):
        a_rows = [dfull[0:SUB]]
        for I in range(1, n_sub):
            s0 = I * SUB
            gb = G[s0 - 1:s0]
            qt = q[s0:s0 + SUB] * jnp.exp2(G[s0:s0 + SUB] - gb)
            kx = jnp.concatenate([k[:s0] * jnp.exp2(gb - G[:s0]), jnp.zeros((CHUNK - s0, HEAD_DIM), F32)], axis=0)
            a_rows.append(_dot(qt, kx, NT) + dfull[s0:s0 + SUB])
        return jnp.concatenate(a_rows, axis=0)

    a_mat = {p: below_diag(G[p], q[p], k[p], dfull[p]) for p in pairs}
    o_intra = {p: _dot(a_mat[p], v[p]) for p in pairs}
    qe = {p: q[p] * jnp.exp2(G[p]) for p in pairs}
    g_last = {p: G[p][CHUNK - 1:CHUNK] for p in pairs}
    kd = {p: k[p] * jnp.exp2(g_last[p] - G[p]) for p in pairs}

    st = [st_ref[h] for h in range(N_HEADS)]
    o_rows = []
    for c in range(n_chunk):
        o_rows.append([o_intra[h, c] + _dot(qe[h, c], st[h], NT) for h in range(N_HEADS)])
        st = [st[h] * jnp.exp2(g_last[h, c]) + _dot(v[h, c], kd[h, c], TN) for h in range(N_HEADS)]
    for h in range(N_HEADS):
        st_ref[h] = st[h]
    ng = ng_ref[...]
    gate = _silu(g_ref[...])
    outs = []
    for h in range(N_HEADS):
        o = jnp.concatenate([o_rows[c][h] for c in range(n_chunk)], axis=0) if n_chunk > 1 else o_rows[0][h]
        o = o * lax.rsqrt(jnp.mean(o * o, axis=-1, keepdims=True) + RMS_EPS) * ng
        outs.append((o * gate[:, h * HEAD_DIM:(h + 1) * HEAD_DIM]).astype(o_ref.dtype))
    o_ref[...] = jnp.concatenate(outs, axis=1)


def _hgrn2_emat():
    e = np.zeros((SUB * HEAD_DIM, CHUNK), np.float32)
    for j in range(SUB):
        for I in range(CHUNK // SUB):
            e[j * HEAD_DIM:(j + 1) * HEAD_DIM, I * SUB + j] = 1.0
    return jnp.asarray(e, BF16)


def _hgrn2(proj, lb, norm_g, B, T):
    tb = _pick(T, (128, 64))
    nt = T // tb
    cb = lambda col: pl.BlockSpec((tb, BRANCH_WIDTH), lambda b, t, col=col: (b * nt + t, col // BRANCH_WIDTH))
    const = lambda shape: pl.BlockSpec(shape, lambda b, t: (0,) * len(shape))
    return pl.pallas_call(
        functools.partial(_hgrn2_kernel, tb=tb), grid=(B, nt),
        in_specs=[cb(COL_HG_Q), cb(COL_HG_F), cb(COL_HG_I), cb(COL_HG_G),
                  const((1, BRANCH_WIDTH)), const((1, HEAD_DIM)), const((SUB * HEAD_DIM, CHUNK))],
        out_specs=pl.BlockSpec((tb, BRANCH_WIDTH), lambda b, t: (b * nt + t, 0)),
        out_shape=jax.ShapeDtypeStruct((B * T, BRANCH_WIDTH), BF16),
        scratch_shapes=[pltpu.VMEM((N_HEADS, HEAD_DIM, HEAD_DIM), F32)],
        compiler_params=_cparams(("parallel", "arbitrary")), name="hgrn2")(
            proj, proj, proj, proj, lb.reshape(1, BRANCH_WIDTH), norm_g.reshape(1, HEAD_DIM), _hgrn2_emat())


def _l2n(x):
    return x * lax.rsqrt(jnp.sum(x * x, axis=-1, keepdims=True) + L2_EPS)


def _gdn_kernel(qkv_ref, prev_ref, z_ref, ab_ref, abt_ref, w_ref, hrow_ref, hcol_ref, ng_ref,
                o_ref, s_ref, *, tb):
    first = pl.program_id(1) == 0

    @pl.when(first)
    def _():
        s_ref[...] = jnp.zeros_like(s_ref)

    x = qkv_ref[...]
    prev = jnp.where(first, 0.0, prev_ref[...])
    xf = jnp.concatenate([prev, x], axis=0)
    w = w_ref[...]
    acc = x * w[GDN_CONV - 1:GDN_CONV]
    for j in range(GDN_CONV - 1):
        off = SUBLANE - (GDN_CONV - 1) + j
        acc = acc + xf[off:off + tb] * w[j:j + 1]
    y = _silu(acc)

    hrow = hrow_ref[...]
    hcol = hcol_ref[...]
    ab = ab_ref[...]
    g_cols = -jnp.exp(hrow[0:1]) * _softplus(ab + hrow[1:2])
    gc_cols = _dot_exact_lhs(_block_tri(tb, True), g_cols)
    beta_cols = _sigmoid(ab)
    abt = abt_ref[...]
    g_rows = -jnp.exp(hcol[:, 0:1]) * _softplus(abt[0:N_HEADS] + hcol[:, LANE:LANE + 1])
    gc_rows = _dot_exact_rhs(g_rows, _block_tri(tb, False))

    ri = lax.broadcasted_iota(jnp.int32, (CHUNK, CHUNK), 0)
    ci = lax.broadcasted_iota(jnp.int32, (CHUNK, CHUNK), 1)
    causal = ri >= ci
    strict = ri > ci
    same_sub = (ri // SUB) == (ci // SUB)
    eye = jnp.where(ri == ci, 1.0, 0.0)
    neg_inf = -jnp.inf
    n_chunk = tb // CHUNK
    pairs = [(h, c) for c in range(n_chunk) for h in range(N_HEADS)]

    def head_cols(base, h):
        return y[:, base + h * HEAD_DIM: base + (h + 1) * HEAD_DIM]

    q_h = [_l2n(head_cols(0, h)) * (HEAD_DIM ** -0.5) for h in range(N_HEADS)]
    k_h = [_l2n(head_cols(BRANCH_WIDTH, h)) for h in range(N_HEADS)]
    v_h = [head_cols(2 * BRANCH_WIDTH, h) for h in range(N_HEADS)]

    rows = lambda c: slice(c * CHUNK, (c + 1) * CHUNK)
    q = {(h, c): q_h[h][rows(c)] for h, c in pairs}
    k = {(h, c): k_h[h][rows(c)] for h, c in pairs}
    v = {(h, c): v_h[h][rows(c)] for h, c in pairs}
    gc = {(h, c): gc_cols[rows(c), h:h + 1] for h, c in pairs}
    bt = {(h, c): beta_cols[rows(c), N_HEADS + h:N_HEADS + h + 1] for h, c in pairs}
    decay = {(h, c): jnp.exp(jnp.where(causal, gc[h, c] - gc_rows[h:h + 1, rows(c)], neg_inf)) for h, c in pairs}
    kb = {p: k[p] * bt[p] for p in pairs}
    L = {p: jnp.where(strict, _dot(kb[p], k[p], NT) * decay[p], 0.0) for p in pairs}
    qk = {p: jnp.where(causal, _dot(q[p], k[p], NT) * decay[p], 0.0) for p in pairs}
    Ld = {p: jnp.where(same_sub, L[p], 0.0) for p in pairs}
    Lo = {p: L[p] - Ld[p] for p in pairs}
    X = {p: eye - Ld[p] for p in pairs}
    P = {p: _dotp(Ld[p], Ld[p], GDN_PASSES["L2"]) for p in pairs}
    for it, (px, pp) in enumerate((("X2", "L4"), ("X4", "L8"), ("X8", None))):
        X = {p: X[p] + _dotp(X[p], P[p], GDN_PASSES[px]) for p in pairs}
        if pp is not None:
            P = {p: _dotp(P[p], P[p], GDN_PASSES[pp]) for p in pairs}
    M = {p: _dotp(X[p], Lo[p], GDN_PASSES["M"]) for p in pairs}
    M2 = {p: _dotp(M[p], M[p], GDN_PASSES["M2"]) for p in pairs}
    Y = {p: (eye - M[p]) + _dotp(eye - M[p], M2[p], GDN_PASSES["Y"]) for p in pairs}
    Tm = {p: _dotp(Y[p], X[p], GDN_PASSES["T"]) for p in pairs}
    uw = {p: _dotp(Tm[p], jnp.concatenate([v[p] * bt[p], kb[p] * jnp.exp(gc[p])], axis=1), GDN_PASSES["UW"])
          for p in pairs}
    qe = {p: q[p] * jnp.exp(gc[p]) for p in pairs}
    g_last = {p: gc[p][CHUNK - 1:CHUNK] for p in pairs}
    kd = {p: k[p] * jnp.exp(g_last[p] - gc[p]) for p in pairs}

    S = [s_ref[h] for h in range(N_HEADS)]
    o_rows = []
    for c in range(n_chunk):
        v_new = [uw[h, c][:, :HEAD_DIM] - _dot(uw[h, c][:, HEAD_DIM:], S[h]) for h in range(N_HEADS)]
        o_c = [_dot(qe[h, c], S[h]) + _dot(qk[h, c], v_new[h]) for h in range(N_HEADS)]
        S = [S[h] * jnp.exp(g_last[h, c]) + _dot(kd[h, c], v_new[h], TN) for h in range(N_HEADS)]
        o_rows.append(o_c)
    for h in range(N_HEADS):
        s_ref[h] = S[h]
    ng = ng_ref[...]
    z = z_ref[...]
    outs = []
    for h in range(N_HEADS):
        o = jnp.concatenate([o_rows[c][h] for c in range(n_chunk)], axis=0) if n_chunk > 1 else o_rows[0][h]
        o = o * lax.rsqrt(jnp.mean(o * o, axis=-1, keepdims=True) + RMS_EPS) * ng
        outs.append((o * _silu(z[:, h * HEAD_DIM:(h + 1) * HEAD_DIM])).astype(o_ref.dtype))
    o_ref[...] = jnp.concatenate(outs, axis=1)


def _gdn(proj, ab_t, conv_w, a_log, dt_bias, norm_g, B, T):
    tb = _pick(T, (128, 64))
    nt = T // tb
    r8 = tb // SUBLANE
    w3 = 3 * BRANCH_WIDTH
    pad = lambda p: jnp.pad(p.astype(F32), (0, LANE - N_HEADS))
    hrow = jnp.stack([pad(a_log), pad(dt_bias)])
    bc = lambda p: jnp.broadcast_to(p.astype(F32)[:, None], (N_HEADS, LANE))
    hcol = jnp.concatenate([bc(a_log), bc(dt_bias)], axis=1)
    const = lambda shape: pl.BlockSpec(shape, lambda b, t: (0,) * len(shape))
    return pl.pallas_call(
        functools.partial(_gdn_kernel, tb=tb), grid=(B, nt),
        in_specs=[pl.BlockSpec((tb, w3), lambda b, t: (b * nt + t, COL_GDN_QKV // w3)),
                  pl.BlockSpec((SUBLANE, w3),
                               lambda b, t: (jnp.maximum((b * nt + t) * r8 - 1, 0), COL_GDN_QKV // w3)),
                  pl.BlockSpec((tb, BRANCH_WIDTH), lambda b, t: (b * nt + t, COL_GDN_Z // BRANCH_WIDTH)),
                  pl.BlockSpec((tb, LANE), lambda b, t: (b * nt + t, COL_AB // LANE)),
                  pl.BlockSpec((2 * N_HEADS, tb), lambda b, t: (0, b * nt + t)),
                  const((GDN_CONV, w3)), const((2, LANE)), const((N_HEADS, 2 * LANE)), const((1, HEAD_DIM))],
        out_specs=pl.BlockSpec((tb, BRANCH_WIDTH), lambda b, t: (b * nt + t, 0)),
        out_shape=jax.ShapeDtypeStruct((B * T, BRANCH_WIDTH), BF16),
        scratch_shapes=[pltpu.VMEM((N_HEADS, HEAD_DIM, HEAD_DIM), F32)],
        compiler_params=_cparams(("parallel", "arbitrary")), name="gated_deltanet")(
            proj, proj, proj, proj, ab_t, conv_w, hrow, hcol, norm_g.reshape(1, HEAD_DIM))


def _swa_kernel(q_ref, kc_ref, vc_ref, kp_ref, vp_ref, sink_ref, o_ref):
    has_prev = pl.program_id(1) > 0
    qi = lax.broadcasted_iota(jnp.int32, (SWA_BLOCK, SWA_BLOCK), 0)
    kj = lax.broadcasted_iota(jnp.int32, (SWA_BLOCK, SWA_BLOCK), 1)
    dist_c = (qi - kj).astype(F32)
    dist_p = dist_c + float(SWA_BLOCK)
    valid_c = qi >= kj
    valid_p = (kj > qi) & has_prev
    neg_inf = -jnp.inf
    group = SWA_Q_HEADS // SWA_KV_HEADS
    q_all = q_ref[...] * (SWA_HEAD_DIM ** -0.5)
    kc, vc, kp, vp = kc_ref[...], vc_ref[...], kp_ref[...], vp_ref[...]
    sinks = sink_ref[...]
    heads = range(SWA_Q_HEADS)
    ks = [slice((hq // group) * SWA_HEAD_DIM, (hq // group + 1) * SWA_HEAD_DIM) for hq in heads]
    slope = [2.0 ** (-8.0 * (hq + 1) / SWA_Q_HEADS) for hq in heads]
    q = [q_all[:, hq * SWA_HEAD_DIM:(hq + 1) * SWA_HEAD_DIM].astype(BF16) for hq in heads]
    kcb, kpb, vcb, vpb = kc.astype(BF16), kp.astype(BF16), vc.astype(BF16), vp.astype(BF16)
    s_c = [jnp.where(valid_c, _dot(q[h], kcb[:, ks[h]], NT) - slope[h] * dist_c, neg_inf) for h in heads]
    s_p = [jnp.where(valid_p, _dot(q[h], kpb[:, ks[h]], NT) - slope[h] * dist_p, neg_inf) for h in heads]
    sink = [sinks[h:h + 1, :1] for h in heads]
    m = [jnp.maximum(jnp.max(jnp.maximum(s_c[h], s_p[h]), axis=-1, keepdims=True), sink[h]) for h in heads]
    p_c = [jnp.exp(s_c[h] - m[h]) for h in heads]
    p_p = [jnp.exp(s_p[h] - m[h]) for h in heads]
    inv = [1.0 / (jnp.sum(p_c[h] + p_p[h], axis=-1, keepdims=True) + jnp.exp(sink[h] - m[h])) for h in heads]
    outs = [_dot(p_c[h] * inv[h], vcb[:, ks[h]]) + _dot(p_p[h] * inv[h], vpb[:, ks[h]]) for h in heads]
    o_ref[...] = jnp.concatenate(outs, axis=1).astype(o_ref.dtype)


def _swa(proj, sinks, B, T):
    nb = T // SWA_BLOCK
    qw = SWA_Q_HEADS * SWA_HEAD_DIM
    cur = lambda col: pl.BlockSpec((SWA_BLOCK, LANE), lambda b, n, col=col: (b * nb + n, col))
    prv = lambda col: pl.BlockSpec((SWA_BLOCK, LANE),
                                   lambda b, n, col=col: (b * nb + jnp.maximum(n - 1, 0), col))
    sink_b = jnp.broadcast_to(sinks.astype(F32)[:, None], (SWA_Q_HEADS, LANE))
    return pl.pallas_call(
        _swa_kernel, grid=(B, nb),
        in_specs=[pl.BlockSpec((SWA_BLOCK, qw), lambda b, n: (b * nb + n, COL_SWA_Q // qw)),
                  cur(COL_SWA_K // LANE), cur(COL_SWA_V // LANE), prv(COL_SWA_K // LANE), prv(COL_SWA_V // LANE),
                  pl.BlockSpec((SWA_Q_HEADS, LANE), lambda b, n: (0, 0))],
        out_specs=pl.BlockSpec((SWA_BLOCK, qw), lambda b, n: (b * nb + n, 0)),
        out_shape=jax.ShapeDtypeStruct((B * T, qw), BF16),
        compiler_params=_cparams(("parallel", "arbitrary")), name="swa")(
            proj, proj, proj, proj, proj, sink_b)


SRC_HG, SRC_QKV, SRC_Z, SRC_AB = 0, 4096, 7168, 8192
N_AB = 2 * N_HEADS
SWA_COLS = COL_AB - COL_SWA_Q


def _w_in_kernel(w_ref, o_ref, *, tk):
    def put(dst, src, n):
        o_ref[dst:dst + n, :] = w_ref[src:src + n, :].astype(BF16)

    put(COL_GDN_QKV, SRC_QKV, 3 * BRANCH_WIDTH)
    put(COL_HG_Q, SRC_HG, 4 * BRANCH_WIDTH)
    put(COL_GDN_Z, SRC_Z, BRANCH_WIDTH)
    put(COL_SWA_Q, SRC_AB + N_AB, SWA_COLS)
    put(COL_AB, SRC_AB, N_AB)
    o_ref[COL_AB + N_AB:COL_GATES, :] = jnp.zeros((COL_GATES - COL_AB - N_AB, tk), BF16)
    put(COL_GATES, SRC_AB + N_AB + SWA_COLS, o_ref.shape[0] - COL_GATES)


def _prep_w_in(w):
    L, d, n_src = w.shape
    n_out = COL_GATES + (n_src - SRC_AB - N_AB - SWA_COLS)
    tk = LANE
    return pl.pallas_call(
        functools.partial(_w_in_kernel, tk=tk), grid=(L, d // tk),
        in_specs=[pl.BlockSpec((None, n_src, tk), lambda l, i: (l, 0, i))],
        out_specs=pl.BlockSpec((None, n_out, tk), lambda l, i: (l, 0, i)),
        out_shape=jax.ShapeDtypeStruct((L, n_out, d), BF16),
        compiler_params=_cparams(("parallel", "parallel")), name="w_in_relayout")(jnp.swapaxes(w, 1, 2))


def kernel(x, ln_in_g, ln_in_b, hg_lb_logits, w_in, gdn_conv_w, gdn_a_log, gdn_dt_bias, hg_norm_g,
           gdn_norm_g, swa_sinks, w_branch, w_out, ln1_g, ln1_b, w_gate_up, w_down, ln2_g, ln2_b):
    B, T, D = x.shape
    M = B * T
    depth = w_in.shape[0]
    lb_all = jnp.cumsum(jax.nn.softmax(hg_lb_logits.astype(F32), axis=0), axis=0)
    lb_all = lb_all - lb_all[0]

    w_in16 = _prep_w_in(w_in)
    w_branch16, w_out16 = w_branch.astype(BF16), w_out.astype(BF16)
    w_down16 = w_down.astype(BF16)

    h32, h16 = _layer_norm(x.reshape(M, D), ln_in_g, ln_in_b)
    for l in range(depth):
        proj = _matmul(h16, w_in16, l, F32, "in_proj", w_is_nk=True, cols=(0, COL_GATES))
        gate_logits = _matmul(h16, w_in16, l, BF16, "in_proj_gates", w_is_nk=True,
                              cols=(COL_GATES, w_in16.shape[1]))
        ab_t = proj[:, COL_AB:COL_AB + N_AB].T
        o_a = _hgrn2(proj, lb_all[l], hg_norm_g[l], B, T)
        o_b = _gdn(proj, ab_t, gdn_conv_w[l], gdn_a_log[l], gdn_dt_bias[l], gdn_norm_g[l], B, T)
        o_c = _swa(proj, swa_sinks[l], B, T)
        merged = _merge(o_a, o_b, o_c, w_branch16, l, gate_logits, D)
        h32, h16 = _matmul_ln(merged, w_out16, l, h32, ln1_g[l], ln1_b[l], "out_proj_ln")
        ff = _glu_matmul(h16, w_gate_up, l)
        ff = _matmul(ff, w_down16, l, BF16, "ffn_down")
        h32, h16 = _layer_norm(ff, ln2_g[l], ln2_b[l], res=h32)
    return h32.reshape(B, T, D)
```

```python
import functools
import math

import jax
import jax.numpy as jnp
import numpy as np
from jax import lax
from jax.experimental import pallas as pl
from jax.experimental.pallas import tpu as pltpu

F32 = jnp.float32
BF16 = jnp.bfloat16

N_HEADS = 8
HEAD_DIM = 128
BRANCH_WIDTH = N_HEADS * HEAD_DIM
GDN_CONV = 4
SWA_Q_HEADS = 16
SWA_KV_HEADS = 2
SWA_HEAD_DIM = 64
SWA_BLOCK = 128
CHUNK = 64
SUB = 16
DEPTH = 2
DEEPNORM_ALPHA = (2 * DEPTH) ** 0.25
LN_EPS = 1e-5
RMS_EPS = 1e-6
L2_EPS = 1e-6
LOG2_E = math.log2(math.e)

COL_GDN_QKV = 0
COL_HG_Q, COL_HG_F, COL_HG_I, COL_HG_G = 3072, 4096, 5120, 6144
COL_GDN_Z = 7168
COL_SWA_Q, COL_SWA_K, COL_SWA_V = 8192, 9216, 9344
COL_AB = 9472
COL_GATES = 10240
LANE = 128
SUBLANE = 8

VMEM_LIMIT = 56 * 1024 * 1024


def _cparams(sem):
    return pltpu.CompilerParams(dimension_semantics=sem, vmem_limit_bytes=VMEM_LIMIT)


def _pick(n, cands):
    for c in cands:
        if n % c == 0:
            return c
    raise ValueError(f"no tile for {n} in {cands}")


def _sigmoid(x):
    return 1.0 / (1.0 + jnp.exp(-x))


def _silu(x):
    return x * _sigmoid(x)


def _log_sigmoid(x):
    return jnp.minimum(x, 0.0) - jnp.log1p(jnp.exp(-jnp.abs(x)))


def _softplus(x):
    return jnp.maximum(x, 0.0) + jnp.log1p(jnp.exp(-jnp.abs(x)))


def _logaddexp(a, b):
    return jnp.maximum(a, b) + jnp.log1p(jnp.exp(-jnp.abs(a - b)))


NN = (((1,), (0,)), ((), ()))
NT = (((1,), (1,)), ((), ()))
TN = (((0,), (0,)), ((), ()))


def _dot(a, b, dims=NN):
    return lax.dot_general(a.astype(BF16), b.astype(BF16), dims, preferred_element_type=F32)


def _split2(x):
    hi = x.astype(BF16)
    lo = (x - hi.astype(F32)).astype(BF16)
    return hi, lo


def _dot3s(a, b, dims=NN):
    dg = functools.partial(lax.dot_general, dimension_numbers=dims, preferred_element_type=F32)
    return dg(a[0], b[0]) + (dg(a[0], b[1]) + dg(a[1], b[0]))


def _dot3(a, b, dims=NN):
    return _dot3s(_split2(a), _split2(b), dims)


def _dotp(a, b, passes):
    return _dot3(a, b) if passes == 3 else _dot(a, b)


GDN_PASSES = {"L2": 1, "X2": 1, "L4": 1, "X4": 1, "L8": 1, "X8": 1, "M": 1, "M2": 1, "Y": 1, "T": 1, "UW": 1}


def _split3(x):
    hi = x.astype(BF16)
    r1 = x - hi.astype(F32)
    mid = r1.astype(BF16)
    lo = (r1 - mid.astype(F32)).astype(BF16)
    return hi, mid, lo


def _dot_exact_lhs(m_bf16, x):
    hi, mid, lo = _split3(x)
    dg = functools.partial(lax.dot_general, dimension_numbers=NN, preferred_element_type=F32)
    return dg(m_bf16, hi) + (dg(m_bf16, mid) + dg(m_bf16, lo))


def _dot_exact_rhs(x, m_bf16):
    hi, mid, lo = _split3(x)
    dg = functools.partial(lax.dot_general, dimension_numbers=NN, preferred_element_type=F32)
    return dg(hi, m_bf16) + (dg(mid, m_bf16) + dg(lo, m_bf16))


def _block_tri(n, lower):
    r = lax.broadcasted_iota(jnp.int32, (n, n), 0)
    c = lax.broadcasted_iota(jnp.int32, (n, n), 1)
    same = (r // CHUNK) == (c // CHUNK)
    tri = (r >= c) if lower else (r <= c)
    return jnp.where(same & tri, 1.0, 0.0).astype(BF16)


def _ln_core(x, g, b):
    mu = jnp.mean(x, axis=-1, keepdims=True)
    xc = x - mu
    var = jnp.mean(xc * xc, axis=-1, keepdims=True)
    return xc * lax.rsqrt(var + LN_EPS) * g + b


def _ln_kernel(x_ref, g_ref, b_ref, o32_ref, o16_ref):
    y = _ln_core(x_ref[...], g_ref[...], b_ref[...])
    o32_ref[...] = y
    o16_ref[...] = y.astype(BF16)


def _ln_res_kernel(h_ref, y_ref, g_ref, b_ref, o32_ref, o16_ref):
    y = _ln_core(DEEPNORM_ALPHA * h_ref[...] + y_ref[...].astype(F32), g_ref[...], b_ref[...])
    o32_ref[...] = y
    o16_ref[...] = y.astype(BF16)


def _layer_norm(x, g, b, res=None):
    M, D = x.shape
    tm = _pick(M, (256, 128, 64, 32, 16))
    row = pl.BlockSpec((tm, D), lambda i: (i, 0))
    vec = pl.BlockSpec((1, D), lambda i: (0, 0))
    g2, b2 = g.reshape(1, D), b.reshape(1, D)
    out_shape = (jax.ShapeDtypeStruct((M, D), F32), jax.ShapeDtypeStruct((M, D), BF16))
    if res is None:
        return pl.pallas_call(_ln_kernel, grid=(M // tm,), in_specs=[row, vec, vec],
                              out_specs=(row, row), out_shape=out_shape,
                              compiler_params=_cparams(("parallel",)), name="layer_norm")(x, g2, b2)
    return pl.pallas_call(_ln_res_kernel, grid=(M // tm,), in_specs=[row, row, vec, vec],
                          out_specs=(row, row), out_shape=out_shape,
                          compiler_params=_cparams(("parallel",)), name="layer_norm_res")(res, x, g2, b2)


def _mm_kernel(a_ref, w_ref, o_ref):
    o_ref[...] = jnp.dot(a_ref[...], w_ref[...], preferred_element_type=F32).astype(o_ref.dtype)


def _mm_nt_kernel(a_ref, w_ref, o_ref):
    o_ref[...] = lax.dot_general(a_ref[...], w_ref[...], NT, preferred_element_type=F32).astype(o_ref.dtype)


def _matmul(a, w, l, out_dtype, name, w_is_nk=False, cols=None):
    M, K = a.shape
    n_all = w.shape[1] if w_is_nk else w.shape[2]
    c0, c1 = cols if cols is not None else (0, n_all)
    N = c1 - c0
    tm = _pick(M, (1024, 512, 256, 128))
    tn = _pick(math.gcd(N, c0) if c0 else N, (1024, 512, 256, 128))
    if K > 8192:
        tm, tn = min(tm, 512), min(tn, 512)
    j0 = c0 // tn
    w_spec = (pl.BlockSpec((None, tn, K), lambda i, j: (l, j0 + j, 0)) if w_is_nk
              else pl.BlockSpec((None, K, tn), lambda i, j: (l, 0, j0 + j)))
    return pl.pallas_call(
        _mm_nt_kernel if w_is_nk else _mm_kernel, grid=(M // tm, N // tn),
        in_specs=[pl.BlockSpec((tm, K), lambda i, j: (i, 0)), w_spec],
        out_specs=pl.BlockSpec((tm, tn), lambda i, j: (i, j)),
        out_shape=jax.ShapeDtypeStruct((M, N), out_dtype),
        compiler_params=_cparams(("parallel", "parallel")), name=name)(a, w)


def _glu_kernel(a_ref, wg_ref, wu_ref, o_ref):
    a = a_ref[...]
    g = jnp.dot(a, wg_ref[...].astype(BF16), preferred_element_type=F32)
    u = jnp.dot(a, wu_ref[...].astype(BF16), preferred_element_type=F32)
    o_ref[...] = (_silu(g) * u).astype(o_ref.dtype)


def _glu_matmul(a, w_gate_up, l):
    M, K = a.shape
    F = w_gate_up.shape[2] // 2
    tm = _pick(M, (1024, 512, 256, 128))
    tn = _pick(F, (512, 256, 128))
    nf = F // tn
    return pl.pallas_call(
        _glu_kernel, grid=(M // tm, nf),
        in_specs=[pl.BlockSpec((tm, K), lambda i, j: (i, 0)),
                  pl.BlockSpec((None, K, tn), lambda i, j: (l, 0, j)),
                  pl.BlockSpec((None, K, tn), lambda i, j: (l, 0, j + nf))],
        out_specs=pl.BlockSpec((tm, tn), lambda i, j: (i, j)),
        out_shape=jax.ShapeDtypeStruct((M, F), BF16),
        compiler_params=_cparams(("parallel", "parallel")), name="ffn_gate_up")(a, w_gate_up, w_gate_up)


def _merge_kernel(oa_ref, ob_ref, oc_ref, wb_ref, g0_ref, g1_ref, g2_ref, o_ref):
    gate = lambda g_ref: _sigmoid(g_ref[...].astype(F32))
    proj = lambda o_ref_b, b: jnp.dot(o_ref_b[...], wb_ref[b], preferred_element_type=F32)
    acc = gate(g0_ref) * proj(oa_ref, 0)
    acc = acc + gate(g1_ref) * proj(ob_ref, 1)
    acc = acc + gate(g2_ref) * proj(oc_ref, 2)
    o_ref[...] = acc.astype(o_ref.dtype)


def _merge(o_a, o_b, o_c, w_branch, l, gate_logits, d_model):
    M = o_a.shape[0]
    tm = _pick(M, (1024, 512, 256, 128))
    tn = _pick(d_model, (512, 256, 128))
    g_step = d_model // tn
    o_spec = pl.BlockSpec((tm, BRANCH_WIDTH), lambda i, j: (i, 0))
    g_specs = [pl.BlockSpec((tm, tn), functools.partial(lambda i, j, b: (i, b * g_step + j), b=b))
               for b in range(3)]
    return pl.pallas_call(
        _merge_kernel, grid=(M // tm, d_model // tn),
        in_specs=[o_spec, o_spec, o_spec,
                  pl.BlockSpec((None, 3, BRANCH_WIDTH, tn), lambda i, j: (l, 0, 0, j))] + g_specs,
        out_specs=pl.BlockSpec((tm, tn), lambda i, j: (i, j)),
        out_shape=jax.ShapeDtypeStruct((M, d_model), BF16),
        compiler_params=_cparams(("parallel", "parallel")), name="branch_merge")(
            o_a, o_b, o_c, w_branch, gate_logits, gate_logits, gate_logits)


def _hgrn2_kernel(q_ref, f_ref, i_ref, g_ref, lb_ref, ng_ref, e_ref, o_ref, st_ref, *, tb):
    @pl.when(pl.program_id(1) == 0)
    def _():
        st_ref[...] = jnp.zeros_like(st_ref)

    lb = lb_ref[...]
    log_lb = jnp.log(lb)
    log1m_lb = jnp.log1p(-lb)
    z = f_ref[...]
    log_f = _logaddexp(log_lb, log1m_lb + _log_sigmoid(z))
    k_all = (1.0 - lb) * _sigmoid(-z)
    q_all = _silu(q_ref[...])
    v_all = i_ref[...]
    g_all = _dot_exact_lhs(_block_tri(tb, True), log_f) * LOG2_E

    row8 = lax.broadcasted_iota(jnp.int32, (SUBLANE, HEAD_DIM), 0)
    same_sub = (lax.broadcasted_iota(jnp.int32, (CHUNK, CHUNK), 0) // SUB
                == lax.broadcasted_iota(jnp.int32, (CHUNK, CHUNK), 1) // SUB)
    e_mat = e_ref[...]
    n_sub = CHUNK // SUB
    n_chunk = tb // CHUNK
    neg_inf = -jnp.inf
    zeros8 = jnp.zeros((SUBLANE, HEAD_DIM), F32)
    pairs = [(h, c) for c in range(n_chunk) for h in range(N_HEADS)]

    def part(x, h, c):
        return x[c * CHUNK:(c + 1) * CHUNK, h * HEAD_DIM:(h + 1) * HEAD_DIM]

    G = {p: part(g_all, *p) for p in pairs}
    q = {p: part(q_all, *p) for p in pairs}
    k = {p: part(k_all, *p) for p in pairs}
    v = {p: part(v_all, *p) for p in pairs}

    def diag_products(G, q, k):
        pcs = []
        for I in range(n_sub):
            s0 = I * SUB
            g_top, g_bot = G[s0:s0 + SUBLANE], G[s0 + SUBLANE:s0 + SUB]
            q_top, q_bot = q[s0:s0 + SUBLANE], q[s0 + SUBLANE:s0 + SUB]
            cols = []
            for j in range(SUB):
                r = s0 + j
                g_r, k_r = G[r:r + 1], k[r:r + 1]
                if j < SUBLANE:
                    top = q_top * jnp.exp2(jnp.where(row8 >= j, g_top - g_r, neg_inf)) * k_r
                    bot = q_bot * jnp.exp2(g_bot - g_r) * k_r
                else:
                    top = zeros8
                    bot = q_bot * jnp.exp2(jnp.where(row8 >= j - SUBLANE, g_bot - g_r, neg_inf)) * k_r
                cols.append(jnp.concatenate([top, bot], axis=0).astype(BF16))
            pcs.append(jnp.concatenate(cols, axis=1))
        return jnp.concatenate(pcs, axis=0)

    pcat = {p: diag_products(G[p], q[p], k[p]) for p in pairs}
    dfull = {p: jnp.where(same_sub, jnp.dot(pcat[p], e_mat, preferred_element_type=F32), 0.0) for p in pairs}

    def below_diag(G, q, k, dfull):
        a_rows = [dfull[0:SUB]]
        for I in range(1, n_sub):
            s0 = I * SUB
            gb = G[s0 - 1:s0]
            qt = q[s0:s0 + SUB] * jnp.exp2(G[s0:s0 + SUB] - gb)
            kx = jnp.concatenate([k[:s0] * jnp.exp2(gb - G[:s0]), jnp.zeros((CHUNK - s0, HEAD_DIM), F32)], axis=0)
            a_rows.append(_dot(qt, kx, NT) + dfull[s0:s0 + SUB])
        return jnp.concatenate(a_rows, axis=0)

    a_mat = {p: below_diag(G[p], q[p], k[p], dfull[p]) for p in pairs}
    o_intra = {p: _dot(a_mat[p], v[p]) for p in pairs}
    qe = {p: q[p] * jnp.exp2(G[p]) for p in pairs}
    g_last = {p: G[p][CHUNK - 1:CHUNK] for p in pairs}
    kd = {p: k[p] * jnp.exp2(g_last[p] - G[p]) for p in pairs}

    st = [st_ref[h] for h in range(N_HEADS)]
    o_rows = []
    for c in range(n_chunk):
        o_rows.append([o_intra[h, c] + _dot(qe[h, c], st[h], NT) for h in range(N_HEADS)])
        st = [st[h] * jnp.exp2(g_last[h, c]) + _dot(v[h, c], kd[h, c], TN) for h in range(N_HEADS)]
    for h in range(N_HEADS):
        st_ref[h] = st[h]
    ng = ng_ref[...]
    gate = _silu(g_ref[...])
    outs = []
    for h in range(N_HEADS):
        o = jnp.concatenate([o_rows[c][h] for c in range(n_chunk)], axis=0) if n_chunk > 1 else o_rows[0][h]
        o = o * lax.rsqrt(jnp.mean(o * o, axis=-1, keepdims=True) + RMS_EPS) * ng
        outs.append((o * gate[:, h * HEAD_DIM:(h + 1) * HEAD_DIM]).astype(o_ref.dtype))
    o_ref[...] = jnp.concatenate(outs, axis=1)


def _hgrn2_emat():
    e = np.zeros((SUB * HEAD_DIM, CHUNK), np.float32)
    for j in range(SUB):
        for I in range(CHUNK // SUB):
            e[j * HEAD_DIM:(j + 1) * HEAD_DIM, I * SUB + j] = 1.0
    return jnp.asarray(e, BF16)


def _hgrn2(proj, lb, norm_g, B, T):
    tb = _pick(T, (128, 64))
    nt = T // tb
    cb = lambda col: pl.BlockSpec((tb, BRANCH_WIDTH), lambda b, t, col=col: (b * nt + t, col // BRANCH_WIDTH))
    const = lambda shape: pl.BlockSpec(shape, lambda b, t: (0,) * len(shape))
    return pl.pallas_call(
        functools.partial(_hgrn2_kernel, tb=tb), grid=(B, nt),
        in_specs=[cb(COL_HG_Q), cb(COL_HG_F), cb(COL_HG_I), cb(COL_HG_G),
                  const((1, BRANCH_WIDTH)), const((1, HEAD_DIM)), const((SUB * HEAD_DIM, CHUNK))],
        out_specs=pl.BlockSpec((tb, BRANCH_WIDTH), lambda b, t: (b * nt + t, 0)),
        out_shape=jax.ShapeDtypeStruct((B * T, BRANCH_WIDTH), BF16),
        scratch_shapes=[pltpu.VMEM((N_HEADS, HEAD_DIM, HEAD_DIM), F32)],
        compiler_params=_cparams(("parallel", "arbitrary")), name="hgrn2")(
            proj, proj, proj, proj, lb.reshape(1, BRANCH_WIDTH), norm_g.reshape(1, HEAD_DIM), _hgrn2_emat())


def _l2n(x):
    return x * lax.rsqrt(jnp.sum(x * x, axis=-1, keepdims=True) + L2_EPS)


def _gdn_kernel(qkv_ref, prev_ref, z_ref, ab_ref, abt_ref, w_ref, hrow_ref, hcol_ref, ng_ref,
                o_ref, s_ref, *, tb):
    first = pl.program_id(1) == 0

    @pl.when(first)
    def _():
        s_ref[...] = jnp.zeros_like(s_ref)

    x = qkv_ref[...]
    prev = jnp.where(first, 0.0, prev_ref[...])
    xf = jnp.concatenate([prev, x], axis=0)
    w = w_ref[...]
    acc = x * w[GDN_CONV - 1:GDN_CONV]
    for j in range(GDN_CONV - 1):
        off = SUBLANE - (GDN_CONV - 1) + j
        acc = acc + xf[off:off + tb] * w[j:j + 1]
    y = _silu(acc)

    hrow = hrow_ref[...]
    hcol = hcol_ref[...]
    ab = ab_ref[...]
    g_cols = -jnp.exp(hrow[0:1]) * _softplus(ab + hrow[1:2])
    gc_cols = _dot_exact_lhs(_block_tri(tb, True), g_cols)
    beta_cols = _sigmoid(ab)
    abt = abt_ref[...]
    g_rows = -jnp.exp(hcol[:, 0:1]) * _softplus(abt[0:N_HEADS] + hcol[:, LANE:LANE + 1])
    gc_rows = _dot_exact_rhs(g_rows, _block_tri(tb, False))

    ri = lax.broadcasted_iota(jnp.int32, (CHUNK, CHUNK), 0)
    ci = lax.broadcasted_iota(jnp.int32, (CHUNK, CHUNK), 1)
    causal = ri >= ci
    strict = ri > ci
    same_sub = (ri // SUB) == (ci // SUB)
    eye = jnp.where(ri == ci, 1.0, 0.0)
    neg_inf = -jnp.inf
    n_chunk = tb // CHUNK
    pairs = [(h, c) for c in range(n_chunk) for h in range(N_HEADS)]

    def head_cols(base, h):
        return y[:, base + h * HEAD_DIM: base + (h + 1) * HEAD_DIM]

    q_h = [_l2n(head_cols(0, h)) * (HEAD_DIM ** -0.5) for h in range(N_HEADS)]
    k_h = [_l2n(head_cols(BRANCH_WIDTH, h)) for h in range(N_HEADS)]
    v_h = [head_cols(2 * BRANCH_WIDTH, h) for h in range(N_HEADS)]

    rows = lambda c: slice(c * CHUNK, (c + 1) * CHUNK)
    q = {(h, c): q_h[h][rows(c)] for h, c in pairs}
    k = {(h, c): k_h[h][rows(c)] for h, c in pairs}
    v = {(h, c): v_h[h][rows(c)] for h, c in pairs}
    gc = {(h, c): gc_cols[rows(c), h:h + 1] for h, c in pairs}
    bt = {(h, c): beta_cols[rows(c), N_HEADS + h:N_HEADS + h + 1] for h, c in pairs}
    decay = {(h, c): jnp.exp(jnp.where(causal, gc[h, c] - gc_rows[h:h + 1, rows(c)], neg_inf)) for h, c in pairs}
    kb = {p: k[p] * bt[p] for p in pairs}
    L = {p: jnp.where(strict, _dot(kb[p], k[p], NT) * decay[p], 0.0) for p in pairs}
    qk = {p: jnp.where(causal, _dot(q[p], k[p], NT) * decay[p], 0.0) for p in pairs}
    Ld = {p: jnp.where(same_sub, L[p], 0.0) for p in pairs}
    Lo = {p: L[p] - Ld[p] for p in pairs}
    X = {p: eye - Ld[p] for p in pairs}
    P = {p: _dotp(Ld[p], Ld[p], GDN_PASSES["L2"]) for p in pairs}
    for it, (px, pp) in enumerate((("X2", "L4"), ("X4", "L8"), ("X8", None))):
        X = {p: X[p] + _dotp(X[p], P[p], GDN_PASSES[px]) for p in pairs}
        if pp is not None:
            P = {p: _dotp(P[p], P[p], GDN_PASSES[pp]) for p in pairs}
    M = {p: _dotp(X[p], Lo[p], GDN_PASSES["M"]) for p in pairs}
    M2 = {p: _dotp(M[p], M[p], GDN_PASSES["M2"]) for p in pairs}
    Y = {p: (eye - M[p]) + _dotp(eye - M[p], M2[p], GDN_PASSES["Y"]) for p in pairs}
    Tm = {p: _dotp(Y[p], X[p], GDN_PASSES["T"]) for p in pairs}
    uw = {p: _dotp(Tm[p], jnp.concatenate([v[p] * bt[p], kb[p] * jnp.exp(gc[p])], axis=1), GDN_PASSES["UW"])
          for p in pairs}
    qe = {p: q[p] * jnp.exp(gc[p]) for p in pairs}
    g_last = {p: gc[p][CHUNK - 1:CHUNK] for p in pairs}
    kd = {p: k[p] * jnp.exp(g_last[p] - gc[p]) for p in pairs}

    S = [s_ref[h] for h in range(N_HEADS)]
    o_rows = []
    for c in range(n_chunk):
        v_new = [uw[h, c][:, :HEAD_DIM] - _dot(uw[h, c][:, HEAD_DIM:], S[h]) for h in range(N_HEADS)]
        o_c = [_dot(qe[h, c], S[h]) + _dot(qk[h, c], v_new[h]) for h in range(N_HEADS)]
        S = [S[h] * jnp.exp(g_last[h, c]) + _dot(kd[h, c], v_new[h], TN) for h in range(N_HEADS)]
        o_rows.append(o_c)
    for h in range(N_HEADS):
        s_ref[h] = S[h]
    ng = ng_ref[...]
    z = z_ref[...]
    outs = []
    for h in range(N_HEADS):
        o = jnp.concatenate([o_rows[c][h] for c in range(n_chunk)], axis=0) if n_chunk > 1 else o_rows[0][h]
        o = o * lax.rsqrt(jnp.mean(o * o, axis=-1, keepdims=True) + RMS_EPS) * ng
        outs.append((o * _silu(z[:, h * HEAD_DIM:(h + 1) * HEAD_DIM])).astype(o_ref.dtype))
    o_ref[...] = jnp.concatenate(outs, axis=1)


def _gdn(proj, ab_t, conv_w, a_log, dt_bias, norm_g, B, T):
    tb = _pick(T, (128, 64))
    nt = T // tb
    r8 = tb // SUBLANE
    w3 = 3 * BRANCH_WIDTH
    pad = lambda p: jnp.pad(p.astype(F32), (0, LANE - N_HEADS))
    hrow = jnp.stack([pad(a_log), pad(dt_bias)])
    bc = lambda p: jnp.broadcast_to(p.astype(F32)[:, None], (N_HEADS, LANE))
    hcol = jnp.concatenate([bc(a_log), bc(dt_bias)], axis=1)
    const = lambda shape: pl.BlockSpec(shape, lambda b, t: (0,) * len(shape))
    return pl.pallas_call(
        functools.partial(_gdn_kernel, tb=tb), grid=(B, nt),
        in_specs=[pl.BlockSpec((tb, w3), lambda b, t: (b * nt + t, COL_GDN_QKV // w3)),
                  pl.BlockSpec((SUBLANE, w3),
                               lambda b, t: (jnp.maximum((b * nt + t) * r8 - 1, 0), COL_GDN_QKV // w3)),
                  pl.BlockSpec((tb, BRANCH_WIDTH), lambda b, t: (b * nt + t, COL_GDN_Z // BRANCH_WIDTH)),
                  pl.BlockSpec((tb, LANE), lambda b, t: (b * nt + t, COL_AB // LANE)),
                  pl.BlockSpec((2 * N_HEADS, tb), lambda b, t: (0, b * nt + t)),
                  const((GDN_CONV, w3)), const((2, LANE)), const((N_HEADS, 2 * LANE)), const((1, HEAD_DIM))],
        out_specs=pl.BlockSpec((tb, BRANCH_WIDTH), lambda b, t: (b * nt + t, 0)),
        out_shape=jax.ShapeDtypeStruct((B * T, BRANCH_WIDTH), BF16),
        scratch_shapes=[pltpu.VMEM((N_HEADS, HEAD_DIM, HEAD_DIM), F32)],
        compiler_params=_cparams(("parallel", "arbitrary")), name="gated_deltanet")(
            proj, proj, proj, proj, ab_t, conv_w, hrow, hcol, norm_g.reshape(1, HEAD_DIM))


def _swa_kernel(q_ref, kc_ref, vc_ref, kp_ref, vp_ref, sink_ref, o_ref):
    has_prev = pl.program_id(1) > 0
    half = SWA_HEAD_DIM
    kj = lax.broadcasted_iota(jnp.int32, (2 * SWA_BLOCK, SWA_BLOCK), 0)
    qi = lax.broadcasted_iota(jnp.int32, (2 * SWA_BLOCK, SWA_BLOCK), 1)
    is_cur = kj < SWA_BLOCK
    valid = (is_cur & (qi >= kj)) | (~is_cur & (kj - SWA_BLOCK > qi) & has_prev)
    dist = jnp.where(valid, jnp.where(is_cur, qi - kj, qi - kj + 2 * SWA_BLOCK).astype(F32), jnp.inf)
    low = lax.broadcasted_iota(jnp.int32, (SWA_BLOCK, LANE), 1) < half
    top = lax.broadcasted_iota(jnp.int32, (LANE, SWA_BLOCK), 0) < half

    q_all = (q_ref[...] * (SWA_HEAD_DIM ** -0.5 * LOG2_E)).astype(BF16)
    swap = lambda x: jnp.concatenate([x[:, half:], x[:, :half]], axis=1)
    k2 = jnp.concatenate([kc_ref[...], kp_ref[...]], axis=0).astype(BF16)
    v2 = jnp.concatenate([vc_ref[...], vp_ref[...]], axis=0).astype(BF16)
    k2 = (k2, swap(k2))
    v2 = (v2, swap(v2))
    sinks = sink_ref[...] * LOG2_E
    heads = range(SWA_Q_HEADS)
    group = SWA_Q_HEADS // SWA_KV_HEADS
    slope2 = [2.0 ** (-8.0 * (hq + 1) / SWA_Q_HEADS) * LOG2_E for hq in heads]
    q = [jnp.where(low if hq % 2 == 0 else ~low, q_all[:, (hq // 2) * LANE:(hq // 2 + 1) * LANE], 0) for hq in heads]
    sel = [(hq % 2) ^ (hq // group) for hq in heads]
    s = [_dot(k2[sel[h]], q[h], NT) - slope2[h] * dist for h in heads]
    sink = [sinks[h:h + 1, :] for h in heads]
    m = [jnp.maximum(jnp.max(s[h], axis=0, keepdims=True), sink[h]) for h in heads]
    p = [jnp.exp2(s[h] - m[h]) for h in heads]
    inv = [1.0 / (jnp.sum(p[h], axis=0, keepdims=True) + jnp.exp2(sink[h] - m[h])) for h in heads]
    o = [_dot(v2[sel[h]], p[h] * inv[h], TN) for h in heads]
    outs = [jnp.where(top, o[2 * t], o[2 * t + 1]).T for t in range(SWA_Q_HEADS // 2)]
    o_ref[...] = jnp.concatenate(outs, axis=1).astype(o_ref.dtype)


def _swa(proj, sinks, B, T):
    nb = T // SWA_BLOCK
    qw = SWA_Q_HEADS * SWA_HEAD_DIM
    cur = lambda col: pl.BlockSpec((SWA_BLOCK, LANE), lambda b, n, col=col: (b * nb + n, col))
    prv = lambda col: pl.BlockSpec((SWA_BLOCK, LANE),
                                   lambda b, n, col=col: (b * nb + jnp.maximum(n - 1, 0), col))
    sink_b = jnp.broadcast_to(sinks.astype(F32)[:, None], (SWA_Q_HEADS, LANE))
    return pl.pallas_call(
        _swa_kernel, grid=(B, nb),
        in_specs=[pl.BlockSpec((SWA_BLOCK, qw), lambda b, n: (b * nb + n, COL_SWA_Q // qw)),
                  cur(COL_SWA_K // LANE), cur(COL_SWA_V // LANE), prv(COL_SWA_K // LANE), prv(COL_SWA_V // LANE),
                  pl.BlockSpec((SWA_Q_HEADS, LANE), lambda b, n: (0, 0))],
        out_specs=pl.BlockSpec((SWA_BLOCK, qw), lambda b, n: (b * nb + n, 0)),
        out_shape=jax.ShapeDtypeStruct((B * T, qw), BF16),
        compiler_params=_cparams(("parallel", "arbitrary")), name="swa")(
            proj, proj, proj, proj, proj, sink_b)


SRC_HG, SRC_QKV, SRC_Z, SRC_AB = 0, 4096, 7168, 8192
N_AB = 2 * N_HEADS
SWA_COLS = COL_AB - COL_SWA_Q


def _w_in_kernel(w_ref, o_ref, *, tk):
    def put(dst, src, n):
        o_ref[dst:dst + n, :] = w_ref[src:src + n, :].astype(BF16)

    put(COL_GDN_QKV, SRC_QKV, 3 * BRANCH_WIDTH)
    put(COL_HG_Q, SRC_HG, 4 * BRANCH_WIDTH)
    put(COL_GDN_Z, SRC_Z, BRANCH_WIDTH)
    put(COL_SWA_Q, SRC_AB + N_AB, SWA_COLS)
    put(COL_AB, SRC_AB, N_AB)
    o_ref[COL_AB + N_AB:COL_GATES, :] = jnp.zeros((COL_GATES - COL_AB - N_AB, tk), BF16)
    put(COL_GATES, SRC_AB + N_AB + SWA_COLS, o_ref.shape[0] - COL_GATES)


def _prep_w_in(w):
    L, d, n_src = w.shape
    n_out = COL_GATES + (n_src - SRC_AB - N_AB - SWA_COLS)
    tk = LANE
    return pl.pallas_call(
        functools.partial(_w_in_kernel, tk=tk), grid=(L, d // tk),
        in_specs=[pl.BlockSpec((None, n_src, tk), lambda l, i: (l, 0, i))],
        out_specs=pl.BlockSpec((None, n_out, tk), lambda l, i: (l, 0, i)),
        out_shape=jax.ShapeDtypeStruct((L, n_out, d), BF16),
        compiler_params=_cparams(("parallel", "parallel")), name="w_in_relayout")(jnp.swapaxes(w, 1, 2))


def kernel(x, ln_in_g, ln_in_b, hg_lb_logits, w_in, gdn_conv_w, gdn_a_log, gdn_dt_bias, hg_norm_g,
           gdn_norm_g, swa_sinks, w_branch, w_out, ln1_g, ln1_b, w_gate_up, w_down, ln2_g, ln2_b):
    B, T, D = x.shape
    M = B * T
    depth = w_in.shape[0]
    lb_all = jnp.cumsum(jax.nn.softmax(hg_lb_logits.astype(F32), axis=0), axis=0)
    lb_all = lb_all - lb_all[0]

    w_in16 = _prep_w_in(w_in)
    w_branch16, w_out16 = w_branch.astype(BF16), w_out.astype(BF16)
    w_down16 = w_down.astype(BF16)

    h32, h16 = _layer_norm(x.reshape(M, D), ln_in_g, ln_in_b)
    for l in range(depth):
        proj = _matmul(h16, w_in16, l, F32, "in_proj", w_is_nk=True, cols=(0, COL_GATES))
        gate_logits = _matmul(h16, w_in16, l, BF16, "in_proj_gates", w_is_nk=True,
                              cols=(COL_GATES, w_in16.shape[1]))
        ab_t = proj[:, COL_AB:COL_AB + N_AB].T
        o_a = _hgrn2(proj, lb_all[l], hg_norm_g[l], B, T)
        o_b = _gdn(proj, ab_t, gdn_conv_w[l], gdn_a_log[l], gdn_dt_bias[l], gdn_norm_g[l], B, T)
        o_c = _swa(proj, swa_sinks[l], B, T)
        merged = _merge(o_a, o_b, o_c, w_branch16, l, gate_logits, D)
        mix = _matmul(merged, w_out16, l, BF16, "out_proj")
        h32, h16 = _layer_norm(mix, ln1_g[l], ln1_b[l], res=h32)
        ff = _glu_matmul(h16, w_gate_up, l)
        ff = _matmul(ff, w_down16, l, BF16, "ffn_down")
        h32, h16 = _layer_norm(ff, ln2_g[l], ln2_b[l], res=h32)
    return h32.reshape(B, T, D)
```

```python
import functools
import math

import jax
import jax.numpy as jnp
import numpy as np
from jax import lax
from jax.experimental import pallas as pl
from jax.experimental.pallas import tpu as pltpu

F32 = jnp.float32
BF16 = jnp.bfloat16

N_HEADS = 8
HEAD_DIM = 128
BRANCH_WIDTH = N_HEADS * HEAD_DIM
GDN_CONV = 4
SWA_Q_HEADS = 16
SWA_KV_HEADS = 2
SWA_HEAD_DIM = 64
SWA_BLOCK = 128
CHUNK = 64
SUB = 16
DEPTH = 2
DEEPNORM_ALPHA = (2 * DEPTH) ** 0.25
LN_EPS = 1e-5
RMS_EPS = 1e-6
L2_EPS = 1e-6
LOG2_E = math.log2(math.e)

COL_GDN_QKV = 0
COL_HG_Q, COL_HG_F, COL_HG_I, COL_HG_G = 3072, 4096, 5120, 6144
COL_GDN_Z = 7168
COL_SWA_Q, COL_SWA_K, COL_SWA_V = 8192, 9216, 9344
COL_AB = 9472
N_PROJ = 9728
LANE = 128
SUBLANE = 8

VMEM_LIMIT = 56 * 1024 * 1024


def _cparams(sem):
    return pltpu.CompilerParams(dimension_semantics=sem, vmem_limit_bytes=VMEM_LIMIT)


def _pick(n, cands):
    for c in cands:
        if n % c == 0:
            return c
    raise ValueError(f"no tile for {n} in {cands}")


def _sigmoid(x):
    return 1.0 / (1.0 + jnp.exp(-x))


def _silu(x):
    return x * _sigmoid(x)


def _log_sigmoid(x):
    return jnp.minimum(x, 0.0) - jnp.log1p(jnp.exp(-jnp.abs(x)))


def _softplus(x):
    return jnp.maximum(x, 0.0) + jnp.log1p(jnp.exp(-jnp.abs(x)))


def _logaddexp(a, b):
    return jnp.maximum(a, b) + jnp.log1p(jnp.exp(-jnp.abs(a - b)))


NN = (((1,), (0,)), ((), ()))
NT = (((1,), (1,)), ((), ()))
TN = (((0,), (0,)), ((), ()))


def _dot(a, b, dims=NN):
    return lax.dot_general(a.astype(BF16), b.astype(BF16), dims, preferred_element_type=F32)


def _split2(x):
    hi = x.astype(BF16)
    lo = (x - hi.astype(F32)).astype(BF16)
    return hi, lo


def _dot3s(a, b, dims=NN):
    dg = functools.partial(lax.dot_general, dimension_numbers=dims, preferred_element_type=F32)
    return dg(a[0], b[0]) + (dg(a[0], b[1]) + dg(a[1], b[0]))


def _dot3(a, b, dims=NN):
    return _dot3s(_split2(a), _split2(b), dims)


def _dotp(a, b, passes):
    return _dot3(a, b) if passes == 3 else _dot(a, b)


GDN_PASSES = {"L2": 1, "X2": 1, "L4": 1, "X4": 1, "L8": 1, "X8": 1, "M": 1, "M2": 1, "Y": 1, "T": 1, "UW": 1}


def _split3(x):
    hi = x.astype(BF16)
    r1 = x - hi.astype(F32)
    mid = r1.astype(BF16)
    lo = (r1 - mid.astype(F32)).astype(BF16)
    return hi, mid, lo


def _dot_exact_lhs(m_bf16, x):
    hi, mid, lo = _split3(x)
    dg = functools.partial(lax.dot_general, dimension_numbers=NN, preferred_element_type=F32)
    return dg(m_bf16, hi) + (dg(m_bf16, mid) + dg(m_bf16, lo))


def _dot_exact_rhs(x, m_bf16):
    hi, mid, lo = _split3(x)
    dg = functools.partial(lax.dot_general, dimension_numbers=NN, preferred_element_type=F32)
    return dg(hi, m_bf16) + (dg(mid, m_bf16) + dg(lo, m_bf16))


def _block_tri(n, lower):
    r = lax.broadcasted_iota(jnp.int32, (n, n), 0)
    c = lax.broadcasted_iota(jnp.int32, (n, n), 1)
    same = (r // CHUNK) == (c // CHUNK)
    tri = (r >= c) if lower else (r <= c)
    return jnp.where(same & tri, 1.0, 0.0).astype(BF16)


def _ln_core(x, g, b):
    mu = jnp.mean(x, axis=-1, keepdims=True)
    xc = x - mu
    var = jnp.mean(xc * xc, axis=-1, keepdims=True)
    return xc * lax.rsqrt(var + LN_EPS) * g + b


def _ln_kernel(x_ref, g_ref, b_ref, o32_ref, o16_ref):
    y = _ln_core(x_ref[...], g_ref[...], b_ref[...])
    o32_ref[...] = y
    o16_ref[...] = y.astype(BF16)


def _ln_res_kernel(h_ref, y_ref, g_ref, b_ref, o32_ref, o16_ref):
    y = _ln_core(DEEPNORM_ALPHA * h_ref[...] + y_ref[...].astype(F32), g_ref[...], b_ref[...])
    o32_ref[...] = y
    o16_ref[...] = y.astype(BF16)


def _layer_norm(x, g, b, res=None):
    M, D = x.shape
    tm = _pick(M, (256, 128, 64, 32, 16))
    row = pl.BlockSpec((tm, D), lambda i: (i, 0))
    vec = pl.BlockSpec((1, D), lambda i: (0, 0))
    g2, b2 = g.reshape(1, D), b.reshape(1, D)
    out_shape = (jax.ShapeDtypeStruct((M, D), F32), jax.ShapeDtypeStruct((M, D), BF16))
    if res is None:
        return pl.pallas_call(_ln_kernel, grid=(M // tm,), in_specs=[row, vec, vec],
                              out_specs=(row, row), out_shape=out_shape,
                              compiler_params=_cparams(("parallel",)), name="layer_norm")(x, g2, b2)
    return pl.pallas_call(_ln_res_kernel, grid=(M // tm,), in_specs=[row, row, vec, vec],
                          out_specs=(row, row), out_shape=out_shape,
                          compiler_params=_cparams(("parallel",)), name="layer_norm_res")(res, x, g2, b2)


def _mm_kernel(a_ref, w_ref, o_ref):
    o_ref[...] = jnp.dot(a_ref[...], w_ref[...], preferred_element_type=F32).astype(o_ref.dtype)


def _mm_nt_kernel(a_ref, w_ref, o_ref):
    o_ref[...] = lax.dot_general(a_ref[...], w_ref[...], NT, preferred_element_type=F32).astype(o_ref.dtype)


def _matmul(a, w, l, out_dtype, name, w_is_nk=False, cols=None):
    M, K = a.shape
    n_all = w.shape[1] if w_is_nk else w.shape[2]
    c0, c1 = cols if cols is not None else (0, n_all)
    N = c1 - c0
    tm = _pick(M, (1024, 512, 256, 128))
    tn = _pick(math.gcd(N, c0) if c0 else N, (1024, 512, 256, 128))
    if K > 8192:
        tm, tn = min(tm, 512), min(tn, 512)
    j0 = c0 // tn
    w_spec = (pl.BlockSpec((None, tn, K), lambda i, j: (l, j0 + j, 0)) if w_is_nk
              else pl.BlockSpec((None, K, tn), lambda i, j: (l, 0, j0 + j)))
    return pl.pallas_call(
        _mm_nt_kernel if w_is_nk else _mm_kernel, grid=(M // tm, N // tn),
        in_specs=[pl.BlockSpec((tm, K), lambda i, j: (i, 0)), w_spec],
        out_specs=pl.BlockSpec((tm, tn), lambda i, j: (i, j)),
        out_shape=jax.ShapeDtypeStruct((M, N), out_dtype),
        compiler_params=_cparams(("parallel", "parallel")), name=name)(a, w)


def _glu_kernel(a_ref, wg_ref, wu_ref, o_ref):
    a = a_ref[...]
    g = jnp.dot(a, wg_ref[...].astype(BF16), preferred_element_type=F32)
    u = jnp.dot(a, wu_ref[...].astype(BF16), preferred_element_type=F32)
    o_ref[...] = (_silu(g) * u).astype(o_ref.dtype)


def _glu_matmul(a, w_gate_up, l):
    M, K = a.shape
    F = w_gate_up.shape[2] // 2
    tm = _pick(M, (2048, 1024, 512, 256, 128))
    tn = _pick(F, (512, 256, 128))
    nf = F // tn
    return pl.pallas_call(
        _glu_kernel, grid=(M // tm, nf),
        in_specs=[pl.BlockSpec((tm, K), lambda i, j: (i, 0)),
                  pl.BlockSpec((None, K, tn), lambda i, j: (l, 0, j)),
                  pl.BlockSpec((None, K, tn), lambda i, j: (l, 0, j + nf))],
        out_specs=pl.BlockSpec((tm, tn), lambda i, j: (i, j)),
        out_shape=jax.ShapeDtypeStruct((M, F), BF16),
        compiler_params=_cparams(("parallel", "parallel")), name="ffn_gate_up")(a, w_gate_up, w_gate_up)


def _merge_kernel(oa_ref, ob_ref, oc_ref, wb_ref, g0_ref, g1_ref, g2_ref, o_ref):
    gate = lambda g_ref: _sigmoid(g_ref[...].astype(F32))
    proj = lambda o_ref_b, b: jnp.dot(o_ref_b[...], wb_ref[b], preferred_element_type=F32)
    acc = gate(g0_ref) * proj(oa_ref, 0)
    acc = acc + gate(g1_ref) * proj(ob_ref, 1)
    acc = acc + gate(g2_ref) * proj(oc_ref, 2)
    o_ref[...] = acc.astype(o_ref.dtype)


def _merge(o_a, o_b, o_c, w_branch, l, gate_logits, d_model):
    M = o_a.shape[0]
    tm = _pick(M, (1024, 512, 256, 128))
    tn = _pick(d_model, (512, 256, 128))
    g_step = d_model // tn
    o_spec = pl.BlockSpec((tm, BRANCH_WIDTH), lambda i, j: (i, 0))
    g_specs = [pl.BlockSpec((tm, tn), functools.partial(lambda i, j, b: (i, b * g_step + j), b=b))
               for b in range(3)]
    return pl.pallas_call(
        _merge_kernel, grid=(M // tm, d_model // tn),
        in_specs=[o_spec, o_spec, o_spec,
                  pl.BlockSpec((None, 3, BRANCH_WIDTH, tn), lambda i, j: (l, 0, 0, j))] + g_specs,
        out_specs=pl.BlockSpec((tm, tn), lambda i, j: (i, j)),
        out_shape=jax.ShapeDtypeStruct((M, d_model), BF16),
        compiler_params=_cparams(("parallel", "parallel")), name="branch_merge")(
            o_a, o_b, o_c, w_branch, gate_logits, gate_logits, gate_logits)


def _hgrn2_kernel(q_ref, f_ref, i_ref, g_ref, lb_ref, ng_ref, e_ref, o_ref, st_ref, *, tb):
    @pl.when(pl.program_id(1) == 0)
    def _():
        st_ref[...] = jnp.zeros_like(st_ref)

    lb = lb_ref[...]
    log_lb = jnp.log(lb)
    log1m_lb = jnp.log1p(-lb)
    z = f_ref[...]
    log_f = _logaddexp(log_lb, log1m_lb + _log_sigmoid(z))
    k_all = (1.0 - lb) * _sigmoid(-z)
    q_all = _silu(q_ref[...])
    v_all = i_ref[...]
    g_all = _dot_exact_lhs(_block_tri(tb, True), log_f) * LOG2_E

    row8 = lax.broadcasted_iota(jnp.int32, (SUBLANE, HEAD_DIM), 0)
    same_sub = (lax.broadcasted_iota(jnp.int32, (CHUNK, CHUNK), 0) // SUB
                == lax.broadcasted_iota(jnp.int32, (CHUNK, CHUNK), 1) // SUB)
    e_mat = e_ref[...]
    n_sub = CHUNK // SUB
    n_chunk = tb // CHUNK
    neg_inf = -jnp.inf
    zeros8 = jnp.zeros((SUBLANE, HEAD_DIM), F32)
    pairs = [(h, c) for c in range(n_chunk) for h in range(N_HEADS)]

    def part(x, h, c):
        return x[c * CHUNK:(c + 1) * CHUNK, h * HEAD_DIM:(h + 1) * HEAD_DIM]

    G = {p: part(g_all, *p) for p in pairs}
    q = {p: part(q_all, *p) for p in pairs}
    k = {p: part(k_all, *p) for p in pairs}
    v = {p: part(v_all, *p) for p in pairs}

    def diag_products(G, q, k):
        pcs = []
        for I in range(n_sub):
            s0 = I * SUB
            g_top, g_bot = G[s0:s0 + SUBLANE], G[s0 + SUBLANE:s0 + SUB]
            q_top, q_bot = q[s0:s0 + SUBLANE], q[s0 + SUBLANE:s0 + SUB]
            cols = []
            for j in range(SUB):
                r = s0 + j
                g_r, k_r = G[r:r + 1], k[r:r + 1]
                if j < SUBLANE:
                    top = q_top * jnp.exp2(jnp.where(row8 >= j, g_top - g_r, neg_inf)) * k_r
                    bot = q_bot * jnp.exp2(g_bot - g_r) * k_r
                else:
                    top = zeros8
                    bot = q_bot * jnp.exp2(jnp.where(row8 >= j - SUBLANE, g_bot - g_r, neg_inf)) * k_r
                cols.append(jnp.concatenate([top, bot], axis=0).astype(BF16))
            pcs.append(jnp.concatenate(cols, axis=1))
        return jnp.concatenate(pcs, axis=0)

    pcat = {p: diag_products(G[p], q[p], k[p]) for p in pairs}
    dfull = {p: jnp.where(same_sub, jnp.dot(pcat[p], e_mat, preferred_element_type=F32), 0.0) for p in pairs}

    def below_diag(G, q, k, dfull):
        a_rows = [dfull[0:SUB]]
        for I in range(1, n_sub):
            s0 = I * SUB
            gb = G[s0 - 1:s0]
            qt = q[s0:s0 + SUB] * jnp.exp2(G[s0:s0 + SUB] - gb)
            kx = jnp.concatenate([k[:s0] * jnp.exp2(gb - G[:s0]), jnp.zeros((CHUNK - s0, HEAD_DIM), F32)], axis=0)
            a_rows.append(_dot(qt, kx, NT) + dfull[s0:s0 + SUB])
        return jnp.concatenate(a_rows, axis=0)

    a_mat = {p: below_diag(G[p], q[p], k[p], dfull[p]) for p in pairs}
    o_intra = {p: _dot(a_mat[p], v[p]) for p in pairs}
    qe = {p: q[p] * jnp.exp2(G[p]) for p in pairs}
    g_last = {p: G[p][CHUNK - 1:CHUNK] for p in pairs}
    kd = {p: k[p] * jnp.exp2(g_last[p] - G[p]) for p in pairs}

    st = [st_ref[h] for h in range(N_HEADS)]
    o_rows = []
    for c in range(n_chunk):
        o_rows.append([o_intra[h, c] + _dot(qe[h, c], st[h], NT) for h in range(N_HEADS)])
        st = [st[h] * jnp.exp2(g_last[h, c]) + _dot(v[h, c], kd[h, c], TN) for h in range(N_HEADS)]
    for h in range(N_HEADS):
        st_ref[h] = st[h]
    ng = ng_ref[...]
    gate = _silu(g_ref[...])
    outs = []
    for h in range(N_HEADS):
        o = jnp.concatenate([o_rows[c][h] for c in range(n_chunk)], axis=0) if n_chunk > 1 else o_rows[0][h]
        o = o * lax.rsqrt(jnp.mean(o * o, axis=-1, keepdims=True) + RMS_EPS) * ng
        outs.append((o * gate[:, h * HEAD_DIM:(h + 1) * HEAD_DIM]).astype(o_ref.dtype))
    o_ref[...] = jnp.concatenate(outs, axis=1)


def _hgrn2_emat():
    e = np.zeros((SUB * HEAD_DIM, CHUNK), np.float32)
    for j in range(SUB):
        for I in range(CHUNK // SUB):
            e[j * HEAD_DIM:(j + 1) * HEAD_DIM, I * SUB + j] = 1.0
    return jnp.asarray(e, BF16)


def _hgrn2(proj, lb, norm_g, B, T):
    tb = _pick(T, (128, 64))
    nt = T // tb
    cb = lambda col: pl.BlockSpec((tb, BRANCH_WIDTH), lambda b, t, col=col: (b * nt + t, col // BRANCH_WIDTH))
    const = lambda shape: pl.BlockSpec(shape, lambda b, t: (0,) * len(shape))
    return pl.pallas_call(
        functools.partial(_hgrn2_kernel, tb=tb), grid=(B, nt),
        in_specs=[cb(COL_HG_Q), cb(COL_HG_F), cb(COL_HG_I), cb(COL_HG_G),
                  const((1, BRANCH_WIDTH)), const((1, HEAD_DIM)), const((SUB * HEAD_DIM, CHUNK))],
        out_specs=pl.BlockSpec((tb, BRANCH_WIDTH), lambda b, t: (b * nt + t, 0)),
        out_shape=jax.ShapeDtypeStruct((B * T, BRANCH_WIDTH), BF16),
        scratch_shapes=[pltpu.VMEM((N_HEADS, HEAD_DIM, HEAD_DIM), F32)],
        compiler_params=_cparams(("parallel", "arbitrary")), name="hgrn2")(
            proj, proj, proj, proj, lb.reshape(1, BRANCH_WIDTH), norm_g.reshape(1, HEAD_DIM), _hgrn2_emat())


def _l2n(x):
    return x * lax.rsqrt(jnp.sum(x * x, axis=-1, keepdims=True) + L2_EPS)


def _gdn_kernel(qkv_ref, prev_ref, z_ref, ab_ref, abt_ref, w_ref, hrow_ref, hcol_ref, ng_ref,
                o_ref, s_ref, *, tb):
    first = pl.program_id(1) == 0

    @pl.when(first)
    def _():
        s_ref[...] = jnp.zeros_like(s_ref)

    x = qkv_ref[...]
    prev = jnp.where(first, 0.0, prev_ref[...])
    xf = jnp.concatenate([prev, x], axis=0)
    w = w_ref[...]
    acc = x * w[GDN_CONV - 1:GDN_CONV]
    for j in range(GDN_CONV - 1):
        off = SUBLANE - (GDN_CONV - 1) + j
        acc = acc + xf[off:off + tb] * w[j:j + 1]
    y = _silu(acc)

    hrow = hrow_ref[...]
    hcol = hcol_ref[...]
    ab = ab_ref[...]
    g_cols = -jnp.exp(hrow[0:1]) * _softplus(ab + hrow[1:2])
    gc_cols = _dot_exact_lhs(_block_tri(tb, True), g_cols)
    beta_cols = _sigmoid(ab)
    abt = abt_ref[...]
    g_rows = -jnp.exp(hcol[:, 0:1]) * _softplus(abt[0:N_HEADS] + hcol[:, LANE:LANE + 1])
    gc_rows = _dot_exact_rhs(g_rows, _block_tri(tb, False))

    ri = lax.broadcasted_iota(jnp.int32, (CHUNK, CHUNK), 0)
    ci = lax.broadcasted_iota(jnp.int32, (CHUNK, CHUNK), 1)
    causal = ri >= ci
    strict = ri > ci
    same_sub = (ri // SUB) == (ci // SUB)
    eye = jnp.where(ri == ci, 1.0, 0.0)
    neg_inf = -jnp.inf
    n_chunk = tb // CHUNK
    pairs = [(h, c) for c in range(n_chunk) for h in range(N_HEADS)]

    def head_cols(base, h):
        return y[:, base + h * HEAD_DIM: base + (h + 1) * HEAD_DIM]

    q_h = [_l2n(head_cols(0, h)) * (HEAD_DIM ** -0.5) for h in range(N_HEADS)]
    k_h = [_l2n(head_cols(BRANCH_WIDTH, h)) for h in range(N_HEADS)]
    v_h = [head_cols(2 * BRANCH_WIDTH, h) for h in range(N_HEADS)]

    rows = lambda c: slice(c * CHUNK, (c + 1) * CHUNK)
    q = {(h, c): q_h[h][rows(c)] for h, c in pairs}
    k = {(h, c): k_h[h][rows(c)] for h, c in pairs}
    v = {(h, c): v_h[h][rows(c)] for h, c in pairs}
    gc = {(h, c): gc_cols[rows(c), h:h + 1] for h, c in pairs}
    bt = {(h, c): beta_cols[rows(c), N_HEADS + h:N_HEADS + h + 1] for h, c in pairs}
    decay = {(h, c): jnp.exp(jnp.where(causal, gc[h, c] - gc_rows[h:h + 1, rows(c)], neg_inf)) for h, c in pairs}
    kb = {p: k[p] * bt[p] for p in pairs}
    L = {p: jnp.where(strict, _dot(kb[p], k[p], NT) * decay[p], 0.0) for p in pairs}
    qk = {p: jnp.where(causal, _dot(q[p], k[p], NT) * decay[p], 0.0) for p in pairs}
    Ld = {p: jnp.where(same_sub, L[p], 0.0) for p in pairs}
    Lo = {p: L[p] - Ld[p] for p in pairs}
    X = {p: eye - Ld[p] for p in pairs}
    P = {p: _dotp(Ld[p], Ld[p], GDN_PASSES["L2"]) for p in pairs}
    for it, (px, pp) in enumerate((("X2", "L4"), ("X4", "L8"), ("X8", None))):
        X = {p: X[p] + _dotp(X[p], P[p], GDN_PASSES[px]) for p in pairs}
        if pp is not None:
            P = {p: _dotp(P[p], P[p], GDN_PASSES[pp]) for p in pairs}
    M = {p: _dotp(X[p], Lo[p], GDN_PASSES["M"]) for p in pairs}
    M2 = {p: _dotp(M[p], M[p], GDN_PASSES["M2"]) for p in pairs}
    Y = {p: (eye - M[p]) + _dotp(eye - M[p], M2[p], GDN_PASSES["Y"]) for p in pairs}
    Tm = {p: _dotp(Y[p], X[p], GDN_PASSES["T"]) for p in pairs}
    uw = {p: _dotp(Tm[p], jnp.concatenate([v[p] * bt[p], kb[p] * jnp.exp(gc[p])], axis=1), GDN_PASSES["UW"])
          for p in pairs}
    qe = {p: q[p] * jnp.exp(gc[p]) for p in pairs}
    g_last = {p: gc[p][CHUNK - 1:CHUNK] for p in pairs}
    kd = {p: k[p] * jnp.exp(g_last[p] - gc[p]) for p in pairs}

    S = [s_ref[h] for h in range(N_HEADS)]
    o_rows = []
    for c in range(n_chunk):
        v_new = [uw[h, c][:, :HEAD_DIM] - _dot(uw[h, c][:, HEAD_DIM:], S[h]) for h in range(N_HEADS)]
        o_c = [_dot(qe[h, c], S[h]) + _dot(qk[h, c], v_new[h]) for h in range(N_HEADS)]
        S = [S[h] * jnp.exp(g_last[h, c]) + _dot(kd[h, c], v_new[h], TN) for h in range(N_HEADS)]
        o_rows.append(o_c)
    for h in range(N_HEADS):
        s_ref[h] = S[h]
    ng = ng_ref[...]
    z = z_ref[...]
    outs = []
    for h in range(N_HEADS):
        o = jnp.concatenate([o_rows[c][h] for c in range(n_chunk)], axis=0) if n_chunk > 1 else o_rows[0][h]
        o = o * lax.rsqrt(jnp.mean(o * o, axis=-1, keepdims=True) + RMS_EPS) * ng
        outs.append((o * _silu(z[:, h * HEAD_DIM:(h + 1) * HEAD_DIM])).astype(o_ref.dtype))
    o_ref[...] = jnp.concatenate(outs, axis=1)


def _gdn(proj, ab_t, conv_w, a_log, dt_bias, norm_g, B, T):
    tb = _pick(T, (128, 64))
    nt = T // tb
    r8 = tb // SUBLANE
    w3 = 3 * BRANCH_WIDTH
    pad = lambda p: jnp.pad(p.astype(F32), (0, LANE - N_HEADS))
    hrow = jnp.stack([pad(a_log), pad(dt_bias)])
    bc = lambda p: jnp.broadcast_to(p.astype(F32)[:, None], (N_HEADS, LANE))
    hcol = jnp.concatenate([bc(a_log), bc(dt_bias)], axis=1)
    const = lambda shape: pl.BlockSpec(shape, lambda b, t: (0,) * len(shape))
    return pl.pallas_call(
        functools.partial(_gdn_kernel, tb=tb), grid=(B, nt),
        in_specs=[pl.BlockSpec((tb, w3), lambda b, t: (b * nt + t, COL_GDN_QKV // w3)),
                  pl.BlockSpec((SUBLANE, w3),
                               lambda b, t: (jnp.maximum((b * nt + t) * r8 - 1, 0), COL_GDN_QKV // w3)),
                  pl.BlockSpec((tb, BRANCH_WIDTH), lambda b, t: (b * nt + t, COL_GDN_Z // BRANCH_WIDTH)),
                  pl.BlockSpec((tb, LANE), lambda b, t: (b * nt + t, COL_AB // LANE)),
                  pl.BlockSpec((2 * N_HEADS, tb), lambda b, t: (0, b * nt + t)),
                  const((GDN_CONV, w3)), const((2, LANE)), const((N_HEADS, 2 * LANE)), const((1, HEAD_DIM))],
        out_specs=pl.BlockSpec((tb, BRANCH_WIDTH), lambda b, t: (b * nt + t, 0)),
        out_shape=jax.ShapeDtypeStruct((B * T, BRANCH_WIDTH), BF16),
        scratch_shapes=[pltpu.VMEM((N_HEADS, HEAD_DIM, HEAD_DIM), F32)],
        compiler_params=_cparams(("parallel", "arbitrary")), name="gated_deltanet")(
            proj, proj, proj, proj, ab_t, conv_w, hrow, hcol, norm_g.reshape(1, HEAD_DIM))


def _swa_kernel(q_ref, kc_ref, vc_ref, kp_ref, vp_ref, sink_ref, o_ref):
    has_prev = pl.program_id(1) > 0
    half = SWA_HEAD_DIM
    kj = lax.broadcasted_iota(jnp.int32, (2 * SWA_BLOCK, SWA_BLOCK), 0)
    qi = lax.broadcasted_iota(jnp.int32, (2 * SWA_BLOCK, SWA_BLOCK), 1)
    is_cur = kj < SWA_BLOCK
    valid = (is_cur & (qi >= kj)) | (~is_cur & (kj - SWA_BLOCK > qi) & has_prev)
    dist = jnp.where(valid, jnp.where(is_cur, qi - kj, qi - kj + 2 * SWA_BLOCK).astype(F32), jnp.inf)
    low = lax.broadcasted_iota(jnp.int32, (SWA_BLOCK, LANE), 1) < half
    top = lax.broadcasted_iota(jnp.int32, (LANE, SWA_BLOCK), 0) < half

    q_all = (q_ref[...] * (SWA_HEAD_DIM ** -0.5 * LOG2_E)).astype(BF16)
    swap = lambda x: jnp.concatenate([x[:, half:], x[:, :half]], axis=1)
    k2 = jnp.concatenate([kc_ref[...], kp_ref[...]], axis=0).astype(BF16)
    v2 = jnp.concatenate([vc_ref[...], vp_ref[...]], axis=0).astype(BF16)
    k2 = (k2, swap(k2))
    v2 = (v2, swap(v2))
    sinks = sink_ref[...] * LOG2_E
    heads = range(SWA_Q_HEADS)
    group = SWA_Q_HEADS // SWA_KV_HEADS
    slope2 = [2.0 ** (-8.0 * (hq + 1) / SWA_Q_HEADS) * LOG2_E for hq in heads]
    q = [jnp.where(low if hq % 2 == 0 else ~low, q_all[:, (hq // 2) * LANE:(hq // 2 + 1) * LANE], 0) for hq in heads]
    sel = [(hq % 2) ^ (hq // group) for hq in heads]
    s = [_dot(k2[sel[h]], q[h], NT) - slope2[h] * dist for h in heads]
    sink = [sinks[h:h + 1, :] for h in heads]
    m = [jnp.maximum(jnp.max(s[h], axis=0, keepdims=True), sink[h]) for h in heads]
    p = [jnp.exp2(s[h] - m[h]) for h in heads]
    inv = [1.0 / (jnp.sum(p[h], axis=0, keepdims=True) + jnp.exp2(sink[h] - m[h])) for h in heads]
    o = [_dot(v2[sel[h]], p[h] * inv[h], TN) for h in heads]
    outs = [jnp.where(top, o[2 * t], o[2 * t + 1]).T for t in range(SWA_Q_HEADS // 2)]
    o_ref[...] = jnp.concatenate(outs, axis=1).astype(o_ref.dtype)


def _swa(proj, sinks, B, T):
    nb = T // SWA_BLOCK
    qw = SWA_Q_HEADS * SWA_HEAD_DIM
    cur = lambda col: pl.BlockSpec((SWA_BLOCK, LANE), lambda b, n, col=col: (b * nb + n, col))
    prv = lambda col: pl.BlockSpec((SWA_BLOCK, LANE),
                                   lambda b, n, col=col: (b * nb + jnp.maximum(n - 1, 0), col))
    sink_b = jnp.broadcast_to(sinks.astype(F32)[:, None], (SWA_Q_HEADS, LANE))
    return pl.pallas_call(
        _swa_kernel, grid=(B, nb),
        in_specs=[pl.BlockSpec((SWA_BLOCK, qw), lambda b, n: (b * nb + n, COL_SWA_Q // qw)),
                  cur(COL_SWA_K // LANE), cur(COL_SWA_V // LANE), prv(COL_SWA_K // LANE), prv(COL_SWA_V // LANE),
                  pl.BlockSpec((SWA_Q_HEADS, LANE), lambda b, n: (0, 0))],
        out_specs=pl.BlockSpec((SWA_BLOCK, qw), lambda b, n: (b * nb + n, 0)),
        out_shape=jax.ShapeDtypeStruct((B * T, qw), BF16),
        compiler_params=_cparams(("parallel", "arbitrary")), name="swa")(
            proj, proj, proj, proj, proj, sink_b)


SRC_HG, SRC_QKV, SRC_Z, SRC_AB = 0, 4096, 7168, 8192
N_AB = 2 * N_HEADS
SWA_COLS = COL_AB - COL_SWA_Q


def _w_in_kernel(w_ref, o_ref, *, tk):
    n_gates = o_ref.shape[0] - N_PROJ

    def put(dst, src, n):
        o_ref[dst:dst + n, :] = w_ref[src:src + n, :].astype(BF16)

    put(0, SRC_AB + N_AB + SWA_COLS, n_gates)
    put(n_gates + COL_GDN_QKV, SRC_QKV, 3 * BRANCH_WIDTH)
    put(n_gates + COL_HG_Q, SRC_HG, 4 * BRANCH_WIDTH)
    put(n_gates + COL_GDN_Z, SRC_Z, BRANCH_WIDTH)
    put(n_gates + COL_SWA_Q, SRC_AB + N_AB, SWA_COLS)
    put(n_gates + COL_AB, SRC_AB, N_AB)
    o_ref[n_gates + COL_AB + N_AB:, :] = jnp.zeros((N_PROJ - COL_AB - N_AB, tk), BF16)


def _prep_w_in(w):
    L, d, n_src = w.shape
    n_out = N_PROJ + (n_src - SRC_AB - N_AB - SWA_COLS)
    tk = LANE
    return pl.pallas_call(
        functools.partial(_w_in_kernel, tk=tk), grid=(L, d // tk),
        in_specs=[pl.BlockSpec((None, n_src, tk), lambda l, i: (l, 0, i))],
        out_specs=pl.BlockSpec((None, n_out, tk), lambda l, i: (l, 0, i)),
        out_shape=jax.ShapeDtypeStruct((L, n_out, d), BF16),
        compiler_params=_cparams(("parallel", "parallel")), name="w_in_relayout")(jnp.swapaxes(w, 1, 2))


def kernel(x, ln_in_g, ln_in_b, hg_lb_logits, w_in, gdn_conv_w, gdn_a_log, gdn_dt_bias, hg_norm_g,
           gdn_norm_g, swa_sinks, w_branch, w_out, ln1_g, ln1_b, w_gate_up, w_down, ln2_g, ln2_b):
    B, T, D = x.shape
    M = B * T
    depth = w_in.shape[0]
    lb_all = jnp.cumsum(jax.nn.softmax(hg_lb_logits.astype(F32), axis=0), axis=0)
    lb_all = lb_all - lb_all[0]

    w_in16 = _prep_w_in(w_in)
    w_branch16, w_out16 = w_branch.astype(BF16), w_out.astype(BF16)
    w_down16 = w_down.astype(BF16)

    h32, h16 = _layer_norm(x.reshape(M, D), ln_in_g, ln_in_b)
    for l in range(depth):
        n_gates = w_in16.shape[1] - N_PROJ
        gate_logits = _matmul(h16, w_in16, l, BF16, "in_proj_gates", w_is_nk=True, cols=(0, n_gates))
        proj = _matmul(h16, w_in16, l, F32, "in_proj", w_is_nk=True, cols=(n_gates, n_gates + N_PROJ))
        ab_t = proj[:, COL_AB:COL_AB + N_AB].T
        o_a = _hgrn2(proj, lb_all[l], hg_norm_g[l], B, T)
        o_b = _gdn(proj, ab_t, gdn_conv_w[l], gdn_a_log[l], gdn_dt_bias[l], gdn_norm_g[l], B, T)
        o_c = _swa(proj, swa_sinks[l], B, T)
        merged = _merge(o_a, o_b, o_c, w_branch16, l, gate_logits, D)
        mix = _matmul(merged, w_out16, l, BF16, "out_proj")
        h32, h16 = _layer_norm(mix, ln1_g[l], ln1_b[l], res=h32)
        ff = _glu_matmul(h16, w_gate_up, l)
        ff = _matmul(ff, w_down16, l, BF16, "ffn_down")
        h32, h16 = _layer_norm(ff, ln2_g[l], ln2_b[l], res=h32)
    return h32.reshape(B, T, D)
```

```python
import functools
import math

import jax
import jax.numpy as jnp
import numpy as np
from jax import lax
from jax.experimental import pallas as pl
from jax.experimental.pallas import tpu as pltpu

F32 = jnp.float32
BF16 = jnp.bfloat16

N_HEADS = 8
HEAD_DIM = 128
BRANCH_WIDTH = N_HEADS * HEAD_DIM
GDN_CONV = 4
SWA_Q_HEADS = 16
SWA_KV_HEADS = 2
SWA_HEAD_DIM = 64
SWA_BLOCK = 128
CHUNK = 64
SUB = 16
DEPTH = 2
DEEPNORM_ALPHA = (2 * DEPTH) ** 0.25
LN_EPS = 1e-5
RMS_EPS = 1e-6
L2_EPS = 1e-6
LOG2_E = math.log2(math.e)

COL_GDN_QKV = 0
COL_HG_Q, COL_HG_F, COL_HG_I, COL_HG_G = 3072, 4096, 5120, 6144
COL_GDN_Z = 7168
COL_SWA_Q, COL_SWA_K, COL_SWA_V = 8192, 9216, 9344
COL_AB = 9472
N_PROJ = 9728
LANE = 128
SUBLANE = 8

VMEM_LIMIT = 56 * 1024 * 1024


def _cparams(sem):
    return pltpu.CompilerParams(dimension_semantics=sem, vmem_limit_bytes=VMEM_LIMIT)


def _pick(n, cands):
    for c in cands:
        if n % c == 0:
            return c
    raise ValueError(f"no tile for {n} in {cands}")


def _sigmoid(x):
    return 1.0 / (1.0 + jnp.exp(-x))


def _silu(x):
    return x * _sigmoid(x)


def _log_sigmoid(x):
    return jnp.minimum(x, 0.0) - jnp.log1p(jnp.exp(-jnp.abs(x)))


def _softplus(x):
    return jnp.maximum(x, 0.0) + jnp.log1p(jnp.exp(-jnp.abs(x)))


def _logaddexp(a, b):
    return jnp.maximum(a, b) + jnp.log1p(jnp.exp(-jnp.abs(a - b)))


NN = (((1,), (0,)), ((), ()))
NT = (((1,), (1,)), ((), ()))
TN = (((0,), (0,)), ((), ()))


def _dot(a, b, dims=NN):
    return lax.dot_general(a.astype(BF16), b.astype(BF16), dims, preferred_element_type=F32)


def _split2(x):
    hi = x.astype(BF16)
    lo = (x - hi.astype(F32)).astype(BF16)
    return hi, lo


def _dot3s(a, b, dims=NN):
    dg = functools.partial(lax.dot_general, dimension_numbers=dims, preferred_element_type=F32)
    return dg(a[0], b[0]) + (dg(a[0], b[1]) + dg(a[1], b[0]))


def _dot3(a, b, dims=NN):
    return _dot3s(_split2(a), _split2(b), dims)


def _dotp(a, b, passes):
    return _dot3(a, b) if passes == 3 else _dot(a, b)


GDN_PASSES = {"L2": 1, "X2": 1, "L4": 1, "X4": 1, "L8": 1, "X8": 1, "M": 1, "M2": 1, "Y": 1, "T": 1, "UW": 1}


def _split3(x):
    hi = x.astype(BF16)
    r1 = x - hi.astype(F32)
    mid = r1.astype(BF16)
    lo = (r1 - mid.astype(F32)).astype(BF16)
    return hi, mid, lo


def _dot_exact_lhs(m_bf16, x):
    hi, mid, lo = _split3(x)
    dg = functools.partial(lax.dot_general, dimension_numbers=NN, preferred_element_type=F32)
    return dg(m_bf16, hi) + (dg(m_bf16, mid) + dg(m_bf16, lo))


def _dot_exact_rhs(x, m_bf16):
    hi, mid, lo = _split3(x)
    dg = functools.partial(lax.dot_general, dimension_numbers=NN, preferred_element_type=F32)
    return dg(hi, m_bf16) + (dg(mid, m_bf16) + dg(lo, m_bf16))


def _block_tri(n, lower):
    r = lax.broadcasted_iota(jnp.int32, (n, n), 0)
    c = lax.broadcasted_iota(jnp.int32, (n, n), 1)
    same = (r // CHUNK) == (c // CHUNK)
    tri = (r >= c) if lower else (r <= c)
    return jnp.where(same & tri, 1.0, 0.0).astype(BF16)


def _ln_core(x, g, b):
    mu = jnp.mean(x, axis=-1, keepdims=True)
    xc = x - mu
    var = jnp.mean(xc * xc, axis=-1, keepdims=True)
    return xc * lax.rsqrt(var + LN_EPS) * g + b


def _ln_kernel(x_ref, g_ref, b_ref, o32_ref, o16_ref):
    y = _ln_core(x_ref[...], g_ref[...], b_ref[...])
    o32_ref[...] = y
    o16_ref[...] = y.astype(BF16)


def _ln_res_kernel(h_ref, y_ref, g_ref, b_ref, o32_ref, o16_ref):
    y = _ln_core(DEEPNORM_ALPHA * h_ref[...] + y_ref[...].astype(F32), g_ref[...], b_ref[...])
    o32_ref[...] = y
    o16_ref[...] = y.astype(BF16)


def _layer_norm(x, g, b, res=None):
    M, D = x.shape
    tm = _pick(M, (256, 128, 64, 32, 16))
    row = pl.BlockSpec((tm, D), lambda i: (i, 0))
    vec = pl.BlockSpec((1, D), lambda i: (0, 0))
    g2, b2 = g.reshape(1, D), b.reshape(1, D)
    out_shape = (jax.ShapeDtypeStruct((M, D), F32), jax.ShapeDtypeStruct((M, D), BF16))
    if res is None:
        return pl.pallas_call(_ln_kernel, grid=(M // tm,), in_specs=[row, vec, vec],
                              out_specs=(row, row), out_shape=out_shape,
                              compiler_params=_cparams(("parallel",)), name="layer_norm")(x, g2, b2)
    return pl.pallas_call(_ln_res_kernel, grid=(M // tm,), in_specs=[row, row, vec, vec],
                          out_specs=(row, row), out_shape=out_shape,
                          compiler_params=_cparams(("parallel",)), name="layer_norm_res")(res, x, g2, b2)


def _mm_kernel(a_ref, w_ref, o_ref):
    o_ref[...] = jnp.dot(a_ref[...], w_ref[...], preferred_element_type=F32).astype(o_ref.dtype)


def _mm_nt_kernel(a_ref, w_ref, o_ref):
    o_ref[...] = lax.dot_general(a_ref[...], w_ref[...], NT, preferred_element_type=F32).astype(o_ref.dtype)


def _matmul(a, w, l, out_dtype, name, w_is_nk=False, cols=None):
    M, K = a.shape
    n_all = w.shape[1] if w_is_nk else w.shape[2]
    c0, c1 = cols if cols is not None else (0, n_all)
    N = c1 - c0
    tm = _pick(M, (1024, 512, 256, 128))
    tn = _pick(math.gcd(N, c0) if c0 else N, (1024, 512, 256, 128))
    if K > 8192:
        tm, tn = min(tm, 512), min(tn, 512)
    j0 = c0 // tn
    w_spec = (pl.BlockSpec((None, tn, K), lambda i, j: (l, j0 + j, 0)) if w_is_nk
              else pl.BlockSpec((None, K, tn), lambda i, j: (l, 0, j0 + j)))
    return pl.pallas_call(
        _mm_nt_kernel if w_is_nk else _mm_kernel, grid=(M // tm, N // tn),
        in_specs=[pl.BlockSpec((tm, K), lambda i, j: (i, 0)), w_spec],
        out_specs=pl.BlockSpec((tm, tn), lambda i, j: (i, j)),
        out_shape=jax.ShapeDtypeStruct((M, N), out_dtype),
        compiler_params=_cparams(("parallel", "parallel")), name=name)(a, w)


def _glu_kernel(a_ref, wg_ref, wu_ref, o_ref):
    a = a_ref[...]
    g = jnp.dot(a, wg_ref[...].astype(BF16), preferred_element_type=F32)
    u = jnp.dot(a, wu_ref[...].astype(BF16), preferred_element_type=F32)
    o_ref[...] = (_silu(g) * u).astype(o_ref.dtype)


def _glu_matmul(a, w_gate_up, l):
    M, K = a.shape
    F = w_gate_up.shape[2] // 2
    tm = _pick(M, (2048, 1024, 512, 256, 128))
    tn = _pick(F, (512, 256, 128))
    nf = F // tn
    return pl.pallas_call(
        _glu_kernel, grid=(M // tm, nf),
        in_specs=[pl.BlockSpec((tm, K), lambda i, j: (i, 0)),
                  pl.BlockSpec((None, K, tn), lambda i, j: (l, 0, j)),
                  pl.BlockSpec((None, K, tn), lambda i, j: (l, 0, j + nf))],
        out_specs=pl.BlockSpec((tm, tn), lambda i, j: (i, j)),
        out_shape=jax.ShapeDtypeStruct((M, F), BF16),
        compiler_params=_cparams(("parallel", "parallel")), name="ffn_gate_up")(a, w_gate_up, w_gate_up)


def _merge_kernel(oa_ref, ob_ref, oc_ref, wb_ref, g0_ref, g1_ref, g2_ref, o_ref):
    gate = lambda g_ref: _sigmoid(g_ref[...].astype(F32))
    proj = lambda o_ref_b, b: jnp.dot(o_ref_b[...], wb_ref[b], preferred_element_type=F32)
    acc = gate(g0_ref) * proj(oa_ref, 0)
    acc = acc + gate(g1_ref) * proj(ob_ref, 1)
    acc = acc + gate(g2_ref) * proj(oc_ref, 2)
    o_ref[...] = acc.astype(o_ref.dtype)


def _merge(o_a, o_b, o_c, w_branch, l, gate_logits, d_model):
    M = o_a.shape[0]
    tm = _pick(M, (1024, 512, 256, 128))
    tn = _pick(d_model, (512, 256, 128))
    g_step = d_model // tn
    o_spec = pl.BlockSpec((tm, BRANCH_WIDTH), lambda i, j: (i, 0))
    g_specs = [pl.BlockSpec((tm, tn), functools.partial(lambda i, j, b: (i, b * g_step + j), b=b))
               for b in range(3)]
    return pl.pallas_call(
        _merge_kernel, grid=(M // tm, d_model // tn),
        in_specs=[o_spec, o_spec, o_spec,
                  pl.BlockSpec((None, 3, BRANCH_WIDTH, tn), lambda i, j: (l, 0, 0, j))] + g_specs,
        out_specs=pl.BlockSpec((tm, tn), lambda i, j: (i, j)),
        out_shape=jax.ShapeDtypeStruct((M, d_model), BF16),
        compiler_params=_cparams(("parallel", "parallel")), name="branch_merge")(
            o_a, o_b, o_c, w_branch, gate_logits, gate_logits, gate_logits)


def _hgrn2_kernel(q_ref, f_ref, i_ref, g_ref, lb_ref, ng_ref, e_ref, o_ref, st_ref, *, tb):
    @pl.when(pl.program_id(1) == 0)
    def _():
        st_ref[...] = jnp.zeros_like(st_ref)

    lb = lb_ref[...]
    log_lb = jnp.log(lb)
    log1m_lb = jnp.log1p(-lb)
    z = f_ref[...]
    log_f = _logaddexp(log_lb, log1m_lb + _log_sigmoid(z))
    k_all = (1.0 - lb) * _sigmoid(-z)
    q_all = _silu(q_ref[...])
    v_all = i_ref[...]
    g_all = _dot_exact_lhs(_block_tri(tb, True), log_f) * LOG2_E

    row8 = lax.broadcasted_iota(jnp.int32, (SUBLANE, HEAD_DIM), 0)
    same_sub = (lax.broadcasted_iota(jnp.int32, (CHUNK, CHUNK), 0) // SUB
                == lax.broadcasted_iota(jnp.int32, (CHUNK, CHUNK), 1) // SUB)
    e_mat = e_ref[...]
    n_sub = CHUNK // SUB
    n_chunk = tb // CHUNK
    neg_inf = -jnp.inf
    zeros8 = jnp.zeros((SUBLANE, HEAD_DIM), F32)
    pairs = [(h, c) for c in range(n_chunk) for h in range(N_HEADS)]

    def part(x, h, c):
        return x[c * CHUNK:(c + 1) * CHUNK, h * HEAD_DIM:(h + 1) * HEAD_DIM]

    G = {p: part(g_all, *p) for p in pairs}
    q = {p: part(q_all, *p) for p in pairs}
    k = {p: part(k_all, *p) for p in pairs}
    v = {p: part(v_all, *p) for p in pairs}

    def diag_products(G, q, k):
        pcs = []
        for I in range(n_sub):
            s0 = I * SUB
            g_top, g_bot = G[s0:s0 + SUBLANE], G[s0 + SUBLANE:s0 + SUB]
            q_top, q_bot = q[s0:s0 + SUBLANE], q[s0 + SUBLANE:s0 + SUB]
            cols = []
            for j in range(SUB):
                r = s0 + j
                g_r, k_r = G[r:r + 1], k[r:r + 1]
                if j < SUBLANE:
                    top = q_top * jnp.exp2(jnp.where(row8 >= j, g_top - g_r, neg_inf)) * k_r
                    bot = q_bot * jnp.exp2(g_bot - g_r) * k_r
                else:
                    top = zeros8
                    bot = q_bot * jnp.exp2(jnp.where(row8 >= j - SUBLANE, g_bot - g_r, neg_inf)) * k_r
                cols.append(jnp.concatenate([top, bot], axis=0).astype(BF16))
            pcs.append(jnp.concatenate(cols, axis=1))
        return jnp.concatenate(pcs, axis=0)

    pcat = {p: diag_products(G[p], q[p], k[p]) for p in pairs}
    dfull = {p: jnp.where(same_sub, jnp.dot(pcat[p], e_mat, preferred_element_type=F32), 0.0) for p in pairs}

    def below_diag(G, q, k, dfull):
        a_rows = [dfull[0:SUB]]
        for I in range(1, n_sub):
            s0 = I * SUB
            gb = G[s0 - 1:s0]
            qt = q[s0:s0 + SUB] * jnp.exp2(G[s0:s0 + SUB] - gb)
            kx = jnp.concatenate([k[:s0] * jnp.exp2(gb - G[:s0]), jnp.zeros((CHUNK - s0, HEAD_DIM), F32)], axis=0)
            a_rows.append(_dot(qt, kx, NT) + dfull[s0:s0 + SUB])
        return jnp.concatenate(a_rows, axis=0)

    a_mat = {p: below_diag(G[p], q[p], k[p], dfull[p]) for p in pairs}
    o_intra = {p: _dot(a_mat[p], v[p]) for p in pairs}
    qe = {p: q[p] * jnp.exp2(G[p]) for p in pairs}
    g_last = {p: G[p][CHUNK - 1:CHUNK] for p in pairs}
    kd = {p: k[p] * jnp.exp2(g_last[p] - G[p]) for p in pairs}

    st = [st_ref[h] for h in range(N_HEADS)]
    o_rows = []
    for c in range(n_chunk):
        o_rows.append([o_intra[h, c] + _dot(qe[h, c], st[h], NT) for h in range(N_HEADS)])
        st = [st[h] * jnp.exp2(g_last[h, c]) + _dot(v[h, c], kd[h, c], TN) for h in range(N_HEADS)]
    for h in range(N_HEADS):
        st_ref[h] = st[h]
    ng = ng_ref[...]
    gate = _silu(g_ref[...])
    outs = []
    for h in range(N_HEADS):
        o = jnp.concatenate([o_rows[c][h] for c in range(n_chunk)], axis=0) if n_chunk > 1 else o_rows[0][h]
        o = o * lax.rsqrt(jnp.mean(o * o, axis=-1, keepdims=True) + RMS_EPS) * ng
        outs.append((o * gate[:, h * HEAD_DIM:(h + 1) * HEAD_DIM]).astype(o_ref.dtype))
    o_ref[...] = jnp.concatenate(outs, axis=1)


def _hgrn2_emat():
    e = np.zeros((SUB * HEAD_DIM, CHUNK), np.float32)
    for j in range(SUB):
        for I in range(CHUNK // SUB):
            e[j * HEAD_DIM:(j + 1) * HEAD_DIM, I * SUB + j] = 1.0
    return jnp.asarray(e, BF16)


def _hgrn2(proj, lb, norm_g, B, T):
    tb = _pick(T, (256, 128, 64))
    nt = T // tb
    cb = lambda col: pl.BlockSpec((tb, BRANCH_WIDTH), lambda b, t, col=col: (b * nt + t, col // BRANCH_WIDTH))
    const = lambda shape: pl.BlockSpec(shape, lambda b, t: (0,) * len(shape))
    return pl.pallas_call(
        functools.partial(_hgrn2_kernel, tb=tb), grid=(B, nt),
        in_specs=[cb(COL_HG_Q), cb(COL_HG_F), cb(COL_HG_I), cb(COL_HG_G),
                  const((1, BRANCH_WIDTH)), const((1, HEAD_DIM)), const((SUB * HEAD_DIM, CHUNK))],
        out_specs=pl.BlockSpec((tb, BRANCH_WIDTH), lambda b, t: (b * nt + t, 0)),
        out_shape=jax.ShapeDtypeStruct((B * T, BRANCH_WIDTH), BF16),
        scratch_shapes=[pltpu.VMEM((N_HEADS, HEAD_DIM, HEAD_DIM), F32)],
        compiler_params=_cparams(("parallel", "arbitrary")), name="hgrn2")(
            proj, proj, proj, proj, lb.reshape(1, BRANCH_WIDTH), norm_g.reshape(1, HEAD_DIM), _hgrn2_emat())


def _l2n(x):
    return x * lax.rsqrt(jnp.sum(x * x, axis=-1, keepdims=True) + L2_EPS)


def _gdn_kernel(qkv_ref, prev_ref, z_ref, ab_ref, abt_ref, w_ref, hrow_ref, hcol_ref, ng_ref,
                o_ref, s_ref, *, tb):
    first = pl.program_id(1) == 0

    @pl.when(first)
    def _():
        s_ref[...] = jnp.zeros_like(s_ref)

    x = qkv_ref[...]
    prev = jnp.where(first, 0.0, prev_ref[...])
    xf = jnp.concatenate([prev, x], axis=0)
    w = w_ref[...]
    acc = x * w[GDN_CONV - 1:GDN_CONV]
    for j in range(GDN_CONV - 1):
        off = SUBLANE - (GDN_CONV - 1) + j
        acc = acc + xf[off:off + tb] * w[j:j + 1]
    y = _silu(acc)

    hrow = hrow_ref[...]
    hcol = hcol_ref[...]
    ab = ab_ref[...]
    g_cols = -jnp.exp(hrow[0:1]) * _softplus(ab + hrow[1:2])
    gc_cols = _dot_exact_lhs(_block_tri(tb, True), g_cols)
    beta_cols = _sigmoid(ab)
    abt = abt_ref[...]
    g_rows = -jnp.exp(hcol[:, 0:1]) * _softplus(abt[0:N_HEADS] + hcol[:, LANE:LANE + 1])
    gc_rows = _dot_exact_rhs(g_rows, _block_tri(tb, False))

    ri = lax.broadcasted_iota(jnp.int32, (CHUNK, CHUNK), 0)
    ci = lax.broadcasted_iota(jnp.int32, (CHUNK, CHUNK), 1)
    causal = ri >= ci
    strict = ri > ci
    same_sub = (ri // SUB) == (ci // SUB)
    eye = jnp.where(ri == ci, 1.0, 0.0)
    neg_inf = -jnp.inf
    n_chunk = tb // CHUNK
    pairs = [(h, c) for c in range(n_chunk) for h in range(N_HEADS)]

    def head_cols(base, h):
        return y[:, base + h * HEAD_DIM: base + (h + 1) * HEAD_DIM]

    q_h = [_l2n(head_cols(0, h)) * (HEAD_DIM ** -0.5) for h in range(N_HEADS)]
    k_h = [_l2n(head_cols(BRANCH_WIDTH, h)) for h in range(N_HEADS)]
    v_h = [head_cols(2 * BRANCH_WIDTH, h) for h in range(N_HEADS)]

    rows = lambda c: slice(c * CHUNK, (c + 1) * CHUNK)
    q = {(h, c): q_h[h][rows(c)] for h, c in pairs}
    k = {(h, c): k_h[h][rows(c)] for h, c in pairs}
    v = {(h, c): v_h[h][rows(c)] for h, c in pairs}
    gc = {(h, c): gc_cols[rows(c), h:h + 1] for h, c in pairs}
    bt = {(h, c): beta_cols[rows(c), N_HEADS + h:N_HEADS + h + 1] for h, c in pairs}
    decay = {(h, c): jnp.exp(jnp.where(causal, gc[h, c] - gc_rows[h:h + 1, rows(c)], neg_inf)) for h, c in pairs}
    kb = {p: k[p] * bt[p] for p in pairs}
    L = {p: jnp.where(strict, _dot(kb[p], k[p], NT) * decay[p], 0.0) for p in pairs}
    qk = {p: jnp.where(causal, _dot(q[p], k[p], NT) * decay[p], 0.0) for p in pairs}
    Ld = {p: jnp.where(same_sub, L[p], 0.0) for p in pairs}
    Lo = {p: L[p] - Ld[p] for p in pairs}
    X = {p: eye - Ld[p] for p in pairs}
    P = {p: _dotp(Ld[p], Ld[p], GDN_PASSES["L2"]) for p in pairs}
    for it, (px, pp) in enumerate((("X2", "L4"), ("X4", "L8"), ("X8", None))):
        X = {p: X[p] + _dotp(X[p], P[p], GDN_PASSES[px]) for p in pairs}
        if pp is not None:
            P = {p: _dotp(P[p], P[p], GDN_PASSES[pp]) for p in pairs}
    M = {p: _dotp(X[p], Lo[p], GDN_PASSES["M"]) for p in pairs}
    M2 = {p: _dotp(M[p], M[p], GDN_PASSES["M2"]) for p in pairs}
    Y = {p: (eye - M[p]) + _dotp(eye - M[p], M2[p], GDN_PASSES["Y"]) for p in pairs}
    Tm = {p: _dotp(Y[p], X[p], GDN_PASSES["T"]) for p in pairs}
    uw = {p: _dotp(Tm[p], jnp.concatenate([v[p] * bt[p], kb[p] * jnp.exp(gc[p])], axis=1), GDN_PASSES["UW"])
          for p in pairs}
    qe = {p: q[p] * jnp.exp(gc[p]) for p in pairs}
    g_last = {p: gc[p][CHUNK - 1:CHUNK] for p in pairs}
    kd = {p: k[p] * jnp.exp(g_last[p] - gc[p]) for p in pairs}

    S = [s_ref[h] for h in range(N_HEADS)]
    o_rows = []
    for c in range(n_chunk):
        v_new = [uw[h, c][:, :HEAD_DIM] - _dot(uw[h, c][:, HEAD_DIM:], S[h]) for h in range(N_HEADS)]
        o_c = [_dot(qe[h, c], S[h]) + _dot(qk[h, c], v_new[h]) for h in range(N_HEADS)]
        S = [S[h] * jnp.exp(g_last[h, c]) + _dot(kd[h, c], v_new[h], TN) for h in range(N_HEADS)]
        o_rows.append(o_c)
    for h in range(N_HEADS):
        s_ref[h] = S[h]
    ng = ng_ref[...]
    z = z_ref[...]
    outs = []
    for h in range(N_HEADS):
        o = jnp.concatenate([o_rows[c][h] for c in range(n_chunk)], axis=0) if n_chunk > 1 else o_rows[0][h]
        o = o * lax.rsqrt(jnp.mean(o * o, axis=-1, keepdims=True) + RMS_EPS) * ng
        outs.append((o * _silu(z[:, h * HEAD_DIM:(h + 1) * HEAD_DIM])).astype(o_ref.dtype))
    o_ref[...] = jnp.concatenate(outs, axis=1)


def _gdn(proj, ab_t, conv_w, a_log, dt_bias, norm_g, B, T):
    tb = _pick(T, (256, 128, 64))
    nt = T // tb
    r8 = tb // SUBLANE
    w3 = 3 * BRANCH_WIDTH
    pad = lambda p: jnp.pad(p.astype(F32), (0, LANE - N_HEADS))
    hrow = jnp.stack([pad(a_log), pad(dt_bias)])
    bc = lambda p: jnp.broadcast_to(p.astype(F32)[:, None], (N_HEADS, LANE))
    hcol = jnp.concatenate([bc(a_log), bc(dt_bias)], axis=1)
    const = lambda shape: pl.BlockSpec(shape, lambda b, t: (0,) * len(shape))
    return pl.pallas_call(
        functools.partial(_gdn_kernel, tb=tb), grid=(B, nt),
        in_specs=[pl.BlockSpec((tb, w3), lambda b, t: (b * nt + t, COL_GDN_QKV // w3)),
                  pl.BlockSpec((SUBLANE, w3),
                               lambda b, t: (jnp.maximum((b * nt + t) * r8 - 1, 0), COL_GDN_QKV // w3)),
                  pl.BlockSpec((tb, BRANCH_WIDTH), lambda b, t: (b * nt + t, COL_GDN_Z // BRANCH_WIDTH)),
                  pl.BlockSpec((tb, LANE), lambda b, t: (b * nt + t, COL_AB // LANE)),
                  pl.BlockSpec((2 * N_HEADS, tb), lambda b, t: (0, b * nt + t)),
                  const((GDN_CONV, w3)), const((2, LANE)), const((N_HEADS, 2 * LANE)), const((1, HEAD_DIM))],
        out_specs=pl.BlockSpec((tb, BRANCH_WIDTH), lambda b, t: (b * nt + t, 0)),
        out_shape=jax.ShapeDtypeStruct((B * T, BRANCH_WIDTH), BF16),
        scratch_shapes=[pltpu.VMEM((N_HEADS, HEAD_DIM, HEAD_DIM), F32)],
        compiler_params=_cparams(("parallel", "arbitrary")), name="gated_deltanet")(
            proj, proj, proj, proj, ab_t, conv_w, hrow, hcol, norm_g.reshape(1, HEAD_DIM))


def _swa_kernel(q_ref, kc_ref, vc_ref, kp_ref, vp_ref, sink_ref, o_ref):
    has_prev = pl.program_id(1) > 0
    half = SWA_HEAD_DIM
    kj = lax.broadcasted_iota(jnp.int32, (2 * SWA_BLOCK, SWA_BLOCK), 0)
    qi = lax.broadcasted_iota(jnp.int32, (2 * SWA_BLOCK, SWA_BLOCK), 1)
    is_cur = kj < SWA_BLOCK
    valid = (is_cur & (qi >= kj)) | (~is_cur & (kj - SWA_BLOCK > qi) & has_prev)
    dist = jnp.where(valid, jnp.where(is_cur, qi - kj, qi - kj + 2 * SWA_BLOCK).astype(F32), jnp.inf)
    low = lax.broadcasted_iota(jnp.int32, (SWA_BLOCK, LANE), 1) < half
    top = lax.broadcasted_iota(jnp.int32, (LANE, SWA_BLOCK), 0) < half

    q_all = (q_ref[...] * (SWA_HEAD_DIM ** -0.5 * LOG2_E)).astype(BF16)
    swap = lambda x: jnp.concatenate([x[:, half:], x[:, :half]], axis=1)
    k2 = jnp.concatenate([kc_ref[...], kp_ref[...]], axis=0).astype(BF16)
    v2 = jnp.concatenate([vc_ref[...], vp_ref[...]], axis=0).astype(BF16)
    k2 = (k2, swap(k2))
    v2 = (v2, swap(v2))
    sinks = sink_ref[...] * LOG2_E
    heads = range(SWA_Q_HEADS)
    group = SWA_Q_HEADS // SWA_KV_HEADS
    slope2 = [2.0 ** (-8.0 * (hq + 1) / SWA_Q_HEADS) * LOG2_E for hq in heads]
    q = [jnp.where(low if hq % 2 == 0 else ~low, q_all[:, (hq // 2) * LANE:(hq // 2 + 1) * LANE], 0) for hq in heads]
    sel = [(hq % 2) ^ (hq // group) for hq in heads]
    s = [_dot(k2[sel[h]], q[h], NT) - slope2[h] * dist for h in heads]
    sink = [sinks[h:h + 1, :] for h in heads]
    m = [jnp.maximum(jnp.max(s[h], axis=0, keepdims=True), sink[h]) for h in heads]
    p = [jnp.exp2(s[h] - m[h]) for h in heads]
    inv = [1.0 / (jnp.sum(p[h], axis=0, keepdims=True) + jnp.exp2(sink[h] - m[h])) for h in heads]
    o = [_dot(v2[sel[h]], p[h] * inv[h], TN) for h in heads]
    outs = [jnp.where(top, o[2 * t], o[2 * t + 1]).T for t in range(SWA_Q_HEADS // 2)]
    o_ref[...] = jnp.concatenate(outs, axis=1).astype(o_ref.dtype)


def _swa(proj, sinks, B, T):
    nb = T // SWA_BLOCK
    qw = SWA_Q_HEADS * SWA_HEAD_DIM
    cur = lambda col: pl.BlockSpec((SWA_BLOCK, LANE), lambda b, n, col=col: (b * nb + n, col))
    prv = lambda col: pl.BlockSpec((SWA_BLOCK, LANE),
                                   lambda b, n, col=col: (b * nb + jnp.maximum(n - 1, 0), col))
    sink_b = jnp.broadcast_to(sinks.astype(F32)[:, None], (SWA_Q_HEADS, LANE))
    return pl.pallas_call(
        _swa_kernel, grid=(B, nb),
        in_specs=[pl.BlockSpec((SWA_BLOCK, qw), lambda b, n: (b * nb + n, COL_SWA_Q // qw)),
                  cur(COL_SWA_K // LANE), cur(COL_SWA_V // LANE), prv(COL_SWA_K // LANE), prv(COL_SWA_V // LANE),
                  pl.BlockSpec((SWA_Q_HEADS, LANE), lambda b, n: (0, 0))],
        out_specs=pl.BlockSpec((SWA_BLOCK, qw), lambda b, n: (b * nb + n, 0)),
        out_shape=jax.ShapeDtypeStruct((B * T, qw), BF16),
        compiler_params=_cparams(("parallel", "arbitrary")), name="swa")(
            proj, proj, proj, proj, proj, sink_b)


SRC_HG, SRC_QKV, SRC_Z, SRC_AB = 0, 4096, 7168, 8192
N_AB = 2 * N_HEADS
SWA_COLS = COL_AB - COL_SWA_Q


def _w_in_kernel(w_ref, o_ref, *, tk):
    n_gates = o_ref.shape[0] - N_PROJ

    def put(dst, src, n):
        o_ref[dst:dst + n, :] = w_ref[src:src + n, :].astype(BF16)

    put(0, SRC_AB + N_AB + SWA_COLS, n_gates)
    put(n_gates + COL_GDN_QKV, SRC_QKV, 3 * BRANCH_WIDTH)
    put(n_gates + COL_HG_Q, SRC_HG, 4 * BRANCH_WIDTH)
    put(n_gates + COL_GDN_Z, SRC_Z, BRANCH_WIDTH)
    put(n_gates + COL_SWA_Q, SRC_AB + N_AB, SWA_COLS)
    put(n_gates + COL_AB, SRC_AB, N_AB)
    o_ref[n_gates + COL_AB + N_AB:, :] = jnp.zeros((N_PROJ - COL_AB - N_AB, tk), BF16)


def _prep_w_in(w):
    L, d, n_src = w.shape
    n_out = N_PROJ + (n_src - SRC_AB - N_AB - SWA_COLS)
    tk = LANE
    return pl.pallas_call(
        functools.partial(_w_in_kernel, tk=tk), grid=(L, d // tk),
        in_specs=[pl.BlockSpec((None, n_src, tk), lambda l, i: (l, 0, i))],
        out_specs=pl.BlockSpec((None, n_out, tk), lambda l, i: (l, 0, i)),
        out_shape=jax.ShapeDtypeStruct((L, n_out, d), BF16),
        compiler_params=_cparams(("parallel", "parallel")), name="w_in_relayout")(jnp.swapaxes(w, 1, 2))


def kernel(x, ln_in_g, ln_in_b, hg_lb_logits, w_in, gdn_conv_w, gdn_a_log, gdn_dt_bias, hg_norm_g,
           gdn_norm_g, swa_sinks, w_branch, w_out, ln1_g, ln1_b, w_gate_up, w_down, ln2_g, ln2_b):
    B, T, D = x.shape
    M = B * T
    depth = w_in.shape[0]
    lb_all = jnp.cumsum(jax.nn.softmax(hg_lb_logits.astype(F32), axis=0), axis=0)
    lb_all = lb_all - lb_all[0]

    w_in16 = _prep_w_in(w_in)
    w_branch16, w_out16 = w_branch.astype(BF16), w_out.astype(BF16)
    w_down16 = w_down.astype(BF16)

    h32, h16 = _layer_norm(x.reshape(M, D), ln_in_g, ln_in_b)
    for l in range(depth):
        n_gates = w_in16.shape[1] - N_PROJ
        gate_logits = _matmul(h16, w_in16, l, BF16, "in_proj_gates", w_is_nk=True, cols=(0, n_gates))
        proj = _matmul(h16, w_in16, l, F32, "in_proj", w_is_nk=True, cols=(n_gates, n_gates + N_PROJ))
        ab_t = proj[:, COL_AB:COL_AB + N_AB].T
        o_a = _hgrn2(proj, lb_all[l], hg_norm_g[l], B, T)
        o_b = _gdn(proj, ab_t, gdn_conv_w[l], gdn_a_log[l], gdn_dt_bias[l], gdn_norm_g[l], B, T)
        o_c = _swa(proj, swa_sinks[l], B, T)
        merged = _merge(o_a, o_b, o_c, w_branch16, l, gate_logits, D)
        mix = _matmul(merged, w_out16, l, BF16, "out_proj")
        h32, h16 = _layer_norm(mix, ln1_g[l], ln1_b[l], res=h32)
        ff = _glu_matmul(h16, w_gate_up, l)
        ff = _matmul(ff, w_down16, l, BF16, "ffn_down")
        h32, h16 = _layer_norm(ff, ln2_g[l], ln2_b[l], res=h32)
    return h32.reshape(B, T, D)
```

```python
import functools
import math

import jax
import jax.numpy as jnp
import numpy as np
from jax import lax
from jax.experimental import pallas as pl
from jax.experimental.pallas import tpu as pltpu

F32 = jnp.float32
BF16 = jnp.bfloat16

N_HEADS = 8
HEAD_DIM = 128
BRANCH_WIDTH = N_HEADS * HEAD_DIM
GDN_CONV = 4
SWA_Q_HEADS = 16
SWA_KV_HEADS = 2
SWA_HEAD_DIM = 64
SWA_BLOCK = 128
CHUNK = 64
SUB = 16
DEPTH = 2
DEEPNORM_ALPHA = (2 * DEPTH) ** 0.25
LN_EPS = 1e-5
RMS_EPS = 1e-6
L2_EPS = 1e-6
LOG2_E = math.log2(math.e)

COL_GDN_QKV = 0
COL_HG_Q, COL_HG_F, COL_HG_I, COL_HG_G = 3072, 4096, 5120, 6144
COL_GDN_Z = 7168
COL_SWA_Q, COL_SWA_K, COL_SWA_V = 8192, 9216, 9344
COL_AB = 9472
N_PROJ = 9728
LANE = 128
SUBLANE = 8

VMEM_LIMIT = 56 * 1024 * 1024


def _cparams(sem):
    return pltpu.CompilerParams(dimension_semantics=sem, vmem_limit_bytes=VMEM_LIMIT)


def _pick(n, cands):
    for c in cands:
        if n % c == 0:
            return c
    raise ValueError(f"no tile for {n} in {cands}")


def _sigmoid(x):
    return 1.0 / (1.0 + jnp.exp(-x))


def _silu(x):
    return x * _sigmoid(x)


def _log_sigmoid(x):
    return jnp.minimum(x, 0.0) - jnp.log1p(jnp.exp(-jnp.abs(x)))


def _softplus(x):
    return jnp.maximum(x, 0.0) + jnp.log1p(jnp.exp(-jnp.abs(x)))


def _logaddexp(a, b):
    return jnp.maximum(a, b) + jnp.log1p(jnp.exp(-jnp.abs(a - b)))


NN = (((1,), (0,)), ((), ()))
NT = (((1,), (1,)), ((), ()))
TN = (((0,), (0,)), ((), ()))


def _dot(a, b, dims=NN):
    return lax.dot_general(a.astype(BF16), b.astype(BF16), dims, preferred_element_type=F32)


def _split3(x):
    hi = x.astype(BF16)
    r1 = x - hi.astype(F32)
    mid = r1.astype(BF16)
    lo = (r1 - mid.astype(F32)).astype(BF16)
    return hi, mid, lo


def _dot_exact_lhs(m_bf16, x):
    hi, mid, lo = _split3(x)
    dg = functools.partial(lax.dot_general, dimension_numbers=NN, preferred_element_type=F32)
    return dg(m_bf16, hi) + (dg(m_bf16, mid) + dg(m_bf16, lo))


def _dot_exact_rhs(x, m_bf16):
    hi, mid, lo = _split3(x)
    dg = functools.partial(lax.dot_general, dimension_numbers=NN, preferred_element_type=F32)
    return dg(hi, m_bf16) + (dg(mid, m_bf16) + dg(lo, m_bf16))


def _block_tri(n, lower):
    r = lax.broadcasted_iota(jnp.int32, (n, n), 0)
    c = lax.broadcasted_iota(jnp.int32, (n, n), 1)
    same = (r // CHUNK) == (c // CHUNK)
    tri = (r >= c) if lower else (r <= c)
    return jnp.where(same & tri, 1.0, 0.0).astype(BF16)


def _ln_core(x, g, b):
    mu = jnp.mean(x, axis=-1, keepdims=True)
    xc = x - mu
    var = jnp.mean(xc * xc, axis=-1, keepdims=True)
    return xc * lax.rsqrt(var + LN_EPS) * g + b


def _ln_kernel(x_ref, g_ref, b_ref, o32_ref, o16_ref):
    y = _ln_core(x_ref[...], g_ref[...], b_ref[...])
    o32_ref[...] = y
    o16_ref[...] = y.astype(BF16)


def _ln_res_kernel(h_ref, y_ref, g_ref, b_ref, o32_ref, o16_ref):
    y = _ln_core(DEEPNORM_ALPHA * h_ref[...] + y_ref[...].astype(F32), g_ref[...], b_ref[...])
    o32_ref[...] = y
    o16_ref[...] = y.astype(BF16)


def _layer_norm(x, g, b, res=None):
    M, D = x.shape
    tm = _pick(M, (256, 128, 64, 32, 16))
    row = pl.BlockSpec((tm, D), lambda i: (i, 0))
    vec = pl.BlockSpec((1, D), lambda i: (0, 0))
    g2, b2 = g.reshape(1, D), b.reshape(1, D)
    out_shape = (jax.ShapeDtypeStruct((M, D), F32), jax.ShapeDtypeStruct((M, D), BF16))
    if res is None:
        return pl.pallas_call(_ln_kernel, grid=(M // tm,), in_specs=[row, vec, vec],
                              out_specs=(row, row), out_shape=out_shape,
                              compiler_params=_cparams(("parallel",)), name="layer_norm")(x, g2, b2)
    return pl.pallas_call(_ln_res_kernel, grid=(M // tm,), in_specs=[row, row, vec, vec],
                          out_specs=(row, row), out_shape=out_shape,
                          compiler_params=_cparams(("parallel",)), name="layer_norm_res")(res, x, g2, b2)


def _mm_kernel(a_ref, w_ref, o_ref):
    o_ref[...] = jnp.dot(a_ref[...], w_ref[...], preferred_element_type=F32).astype(o_ref.dtype)


def _mm_nt_kernel(a_ref, w_ref, o_ref):
    o_ref[...] = lax.dot_general(a_ref[...], w_ref[...], NT, preferred_element_type=F32).astype(o_ref.dtype)


def _matmul(a, w, l, out_dtype, name, w_is_nk=False, cols=None):
    M, K = a.shape
    n_all = w.shape[1] if w_is_nk else w.shape[2]
    c0, c1 = cols if cols is not None else (0, n_all)
    N = c1 - c0
    tm = _pick(M, (1024, 512, 256, 128))
    tn = _pick(math.gcd(N, c0) if c0 else N, (1024, 512, 256, 128))
    if K > 8192:
        tm, tn = min(tm, 512), min(tn, 512)
    j0 = c0 // tn
    w_spec = (pl.BlockSpec((None, tn, K), lambda i, j: (l, j0 + j, 0)) if w_is_nk
              else pl.BlockSpec((None, K, tn), lambda i, j: (l, 0, j0 + j)))
    return pl.pallas_call(
        _mm_nt_kernel if w_is_nk else _mm_kernel, grid=(M // tm, N // tn),
        in_specs=[pl.BlockSpec((tm, K), lambda i, j: (i, 0)), w_spec],
        out_specs=pl.BlockSpec((tm, tn), lambda i, j: (i, j)),
        out_shape=jax.ShapeDtypeStruct((M, N), out_dtype),
        compiler_params=_cparams(("parallel", "parallel")), name=name)(a, w)


def _glu_kernel(a_ref, wg_ref, wu_ref, o_ref):
    a = a_ref[...]
    g = jnp.dot(a, wg_ref[...].astype(BF16), preferred_element_type=F32)
    u = jnp.dot(a, wu_ref[...].astype(BF16), preferred_element_type=F32)
    o_ref[...] = (_silu(g) * u).astype(o_ref.dtype)


def _glu_matmul(a, w_gate_up, l):
    M, K = a.shape
    F = w_gate_up.shape[2] // 2
    tm = _pick(M, (2048, 1024, 512, 256, 128))
    tn = _pick(F, (512, 256, 128))
    nf = F // tn
    return pl.pallas_call(
        _glu_kernel, grid=(M // tm, nf),
        in_specs=[pl.BlockSpec((tm, K), lambda i, j: (i, 0)),
                  pl.BlockSpec((None, K, tn), lambda i, j: (l, 0, j)),
                  pl.BlockSpec((None, K, tn), lambda i, j: (l, 0, j + nf))],
        out_specs=pl.BlockSpec((tm, tn), lambda i, j: (i, j)),
        out_shape=jax.ShapeDtypeStruct((M, F), BF16),
        compiler_params=_cparams(("parallel", "parallel")), name="ffn_gate_up")(a, w_gate_up, w_gate_up)


def _merge_kernel(oa_ref, ob_ref, oc_ref, wb_ref, g0_ref, g1_ref, g2_ref, o_ref):
    gate = lambda g_ref: _sigmoid(g_ref[...].astype(F32))
    proj = lambda o_ref_b, b: jnp.dot(o_ref_b[...], wb_ref[b], preferred_element_type=F32)
    acc = gate(g0_ref) * proj(oa_ref, 0)
    acc = acc + gate(g1_ref) * proj(ob_ref, 1)
    acc = acc + gate(g2_ref) * proj(oc_ref, 2)
    o_ref[...] = acc.astype(o_ref.dtype)


def _merge(o_a, o_b, o_c, w_branch, l, gate_logits, d_model):
    M = o_a.shape[0]
    tm = _pick(M, (1024, 512, 256, 128))
    tn = _pick(d_model, (512, 256, 128))
    g_step = d_model // tn
    o_spec = pl.BlockSpec((tm, BRANCH_WIDTH), lambda i, j: (i, 0))
    g_specs = [pl.BlockSpec((tm, tn), functools.partial(lambda i, j, b: (i, b * g_step + j), b=b))
               for b in range(3)]
    return pl.pallas_call(
        _merge_kernel, grid=(M // tm, d_model // tn),
        in_specs=[o_spec, o_spec, o_spec,
                  pl.BlockSpec((None, 3, BRANCH_WIDTH, tn), lambda i, j: (l, 0, 0, j))] + g_specs,
        out_specs=pl.BlockSpec((tm, tn), lambda i, j: (i, j)),
        out_shape=jax.ShapeDtypeStruct((M, d_model), BF16),
        compiler_params=_cparams(("parallel", "parallel")), name="branch_merge")(
            o_a, o_b, o_c, w_branch, gate_logits, gate_logits, gate_logits)


def _hgrn2_kernel(q_ref, f_ref, i_ref, g_ref, lb_ref, ng_ref, e_ref, o_ref, st_ref, *, tb):
    @pl.when(pl.program_id(1) == 0)
    def _():
        st_ref[...] = jnp.zeros_like(st_ref)

    lb = lb_ref[...]
    log_lb = jnp.log(lb)
    log1m_lb = jnp.log1p(-lb)
    z = f_ref[...]
    log_f = _logaddexp(log_lb, log1m_lb + _log_sigmoid(z))
    k_all = (1.0 - lb) * _sigmoid(-z)
    q_all = _silu(q_ref[...])
    v_all = i_ref[...]
    g_all = _dot_exact_lhs(_block_tri(tb, True), log_f) * LOG2_E

    row8 = lax.broadcasted_iota(jnp.int32, (SUBLANE, HEAD_DIM), 0)
    same_sub = (lax.broadcasted_iota(jnp.int32, (CHUNK, CHUNK), 0) // SUB
                == lax.broadcasted_iota(jnp.int32, (CHUNK, CHUNK), 1) // SUB)
    e_mat = e_ref[...]
    n_sub = CHUNK // SUB
    n_chunk = tb // CHUNK
    neg_inf = -jnp.inf
    zeros8 = jnp.zeros((SUBLANE, HEAD_DIM), F32)
    pairs = [(h, c) for c in range(n_chunk) for h in range(N_HEADS)]

    def part(x, h, c):
        return x[c * CHUNK:(c + 1) * CHUNK, h * HEAD_DIM:(h + 1) * HEAD_DIM]

    G = {p: part(g_all, *p) for p in pairs}
    q = {p: part(q_all, *p) for p in pairs}
    k = {p: part(k_all, *p) for p in pairs}
    v = {p: part(v_all, *p) for p in pairs}

    def diag_products(G, q, k):
        pcs = []
        for I in range(n_sub):
            s0 = I * SUB
            g_top, g_bot = G[s0:s0 + SUBLANE], G[s0 + SUBLANE:s0 + SUB]
            q_top, q_bot = q[s0:s0 + SUBLANE], q[s0 + SUBLANE:s0 + SUB]
            cols = []
            for j in range(SUB):
                r = s0 + j
                g_r, k_r = G[r:r + 1], k[r:r + 1]
                if j < SUBLANE:
                    top = q_top * jnp.exp2(jnp.where(row8 >= j, g_top - g_r, neg_inf)) * k_r
                    bot = q_bot * jnp.exp2(g_bot - g_r) * k_r
                else:
                    top = zeros8
                    bot = q_bot * jnp.exp2(jnp.where(row8 >= j - SUBLANE, g_bot - g_r, neg_inf)) * k_r
                cols.append(jnp.concatenate([top, bot], axis=0).astype(BF16))
            pcs.append(jnp.concatenate(cols, axis=1))
        return jnp.concatenate(pcs, axis=0)

    pcat = {p: diag_products(G[p], q[p], k[p]) for p in pairs}
    dfull = {p: jnp.where(same_sub, jnp.dot(pcat[p], e_mat, preferred_element_type=F32), 0.0) for p in pairs}

    def below_diag(G, q, k, dfull):
        a_rows = [dfull[0:SUB]]
        for I in range(1, n_sub):
            s0 = I * SUB
            gb = G[s0 - 1:s0]
            qt = q[s0:s0 + SUB] * jnp.exp2(G[s0:s0 + SUB] - gb)
            kx = jnp.concatenate([k[:s0] * jnp.exp2(gb - G[:s0]), jnp.zeros((CHUNK - s0, HEAD_DIM), F32)], axis=0)
            a_rows.append(_dot(qt, kx, NT) + dfull[s0:s0 + SUB])
        return jnp.concatenate(a_rows, axis=0)

    a_mat = {p: below_diag(G[p], q[p], k[p], dfull[p]) for p in pairs}
    o_intra = {p: _dot(a_mat[p], v[p]) for p in pairs}
    qe = {p: q[p] * jnp.exp2(G[p]) for p in pairs}
    g_last = {p: G[p][CHUNK - 1:CHUNK] for p in pairs}
    kd = {p: k[p] * jnp.exp2(g_last[p] - G[p]) for p in pairs}

    st = [st_ref[h] for h in range(N_HEADS)]
    o_rows = []
    for c in range(n_chunk):
        o_rows.append([o_intra[h, c] + _dot(qe[h, c], st[h], NT) for h in range(N_HEADS)])
        st = [st[h] * jnp.exp2(g_last[h, c]) + _dot(v[h, c], kd[h, c], TN) for h in range(N_HEADS)]
    for h in range(N_HEADS):
        st_ref[h] = st[h]
    ng = ng_ref[...]
    gate = _silu(g_ref[...])
    outs = []
    for h in range(N_HEADS):
        o = jnp.concatenate([o_rows[c][h] for c in range(n_chunk)], axis=0) if n_chunk > 1 else o_rows[0][h]
        o = o * lax.rsqrt(jnp.mean(o * o, axis=-1, keepdims=True) + RMS_EPS) * ng
        outs.append((o * gate[:, h * HEAD_DIM:(h + 1) * HEAD_DIM]).astype(o_ref.dtype))
    o_ref[...] = jnp.concatenate(outs, axis=1)


def _hgrn2_emat():
    e = np.zeros((SUB * HEAD_DIM, CHUNK), np.float32)
    for j in range(SUB):
        for I in range(CHUNK // SUB):
            e[j * HEAD_DIM:(j + 1) * HEAD_DIM, I * SUB + j] = 1.0
    return jnp.asarray(e, BF16)


def _hgrn2(proj, lb, norm_g, B, T):
    tb = _pick(T, (256, 128, 64))
    nt = T // tb
    cb = lambda col: pl.BlockSpec((tb, BRANCH_WIDTH), lambda b, t, col=col: (b * nt + t, col // BRANCH_WIDTH))
    const = lambda shape: pl.BlockSpec(shape, lambda b, t: (0,) * len(shape))
    return pl.pallas_call(
        functools.partial(_hgrn2_kernel, tb=tb), grid=(B, nt),
        in_specs=[cb(COL_HG_Q), cb(COL_HG_F), cb(COL_HG_I), cb(COL_HG_G),
                  const((1, BRANCH_WIDTH)), const((1, HEAD_DIM)), const((SUB * HEAD_DIM, CHUNK))],
        out_specs=pl.BlockSpec((tb, BRANCH_WIDTH), lambda b, t: (b * nt + t, 0)),
        out_shape=jax.ShapeDtypeStruct((B * T, BRANCH_WIDTH), BF16),
        scratch_shapes=[pltpu.VMEM((N_HEADS, HEAD_DIM, HEAD_DIM), F32)],
        compiler_params=_cparams(("parallel", "arbitrary")), name="hgrn2")(
            proj, proj, proj, proj, lb.reshape(1, BRANCH_WIDTH), norm_g.reshape(1, HEAD_DIM), _hgrn2_emat())


def _l2n(x):
    return x * lax.rsqrt(jnp.sum(x * x, axis=-1, keepdims=True) + L2_EPS)


def _gdn_kernel(qkv_ref, prev_ref, z_ref, ab_ref, abt_ref, w_ref, hrow_ref, hcol_ref, ng_ref,
                o_ref, s_ref, *, tb):
    first = pl.program_id(1) == 0

    @pl.when(first)
    def _():
        s_ref[...] = jnp.zeros_like(s_ref)

    x = qkv_ref[...]
    prev = jnp.where(first, 0.0, prev_ref[...])
    xf = jnp.concatenate([prev, x], axis=0)
    w = w_ref[...]
    acc = x * w[GDN_CONV - 1:GDN_CONV]
    for j in range(GDN_CONV - 1):
        off = SUBLANE - (GDN_CONV - 1) + j
        acc = acc + xf[off:off + tb] * w[j:j + 1]
    y = _silu(acc)

    hrow = hrow_ref[...]
    hcol = hcol_ref[...]
    ab = ab_ref[...]
    g_cols = -jnp.exp(hrow[0:1]) * _softplus(ab + hrow[1:2])
    gc_cols = _dot_exact_lhs(_block_tri(tb, True), g_cols)
    beta_cols = _sigmoid(ab)
    abt = abt_ref[...]
    g_rows = -jnp.exp(hcol[:, 0:1]) * _softplus(abt[0:N_HEADS] + hcol[:, LANE:LANE + 1])
    gc_rows = _dot_exact_rhs(g_rows, _block_tri(tb, False))

    ri = lax.broadcasted_iota(jnp.int32, (CHUNK, CHUNK), 0)
    ci = lax.broadcasted_iota(jnp.int32, (CHUNK, CHUNK), 1)
    causal = ri >= ci
    strict = ri > ci
    same_sub = (ri // SUB) == (ci // SUB)
    eye = jnp.where(ri == ci, 1.0, 0.0)
    neg_inf = -jnp.inf
    n_chunk = tb // CHUNK
    pairs = [(h, c) for c in range(n_chunk) for h in range(N_HEADS)]

    def head_cols(base, h):
        return y[:, base + h * HEAD_DIM: base + (h + 1) * HEAD_DIM]

    q_h = [_l2n(head_cols(0, h)) * (HEAD_DIM ** -0.5) for h in range(N_HEADS)]
    k_h = [_l2n(head_cols(BRANCH_WIDTH, h)) for h in range(N_HEADS)]
    v_h = [head_cols(2 * BRANCH_WIDTH, h) for h in range(N_HEADS)]

    rows = lambda c: slice(c * CHUNK, (c + 1) * CHUNK)
    q = {(h, c): q_h[h][rows(c)] for h, c in pairs}
    k = {(h, c): k_h[h][rows(c)] for h, c in pairs}
    v = {(h, c): v_h[h][rows(c)] for h, c in pairs}
    gc = {(h, c): gc_cols[rows(c), h:h + 1] for h, c in pairs}
    bt = {(h, c): beta_cols[rows(c), N_HEADS + h:N_HEADS + h + 1] for h, c in pairs}
    decay = {(h, c): jnp.exp(jnp.where(causal, gc[h, c] - gc_rows[h:h + 1, rows(c)], neg_inf)) for h, c in pairs}
    kb = {p: k[p] * bt[p] for p in pairs}
    L = {p: jnp.where(strict, _dot(kb[p], k[p], NT) * decay[p], 0.0) for p in pairs}
    qk = {p: jnp.where(causal, _dot(q[p], k[p], NT) * decay[p], 0.0) for p in pairs}
    Ld = {p: jnp.where(same_sub, L[p], 0.0) for p in pairs}
    Lo = {p: L[p] - Ld[p] for p in pairs}
    X = {p: eye - Ld[p] for p in pairs}
    P = {p: _dot(Ld[p], Ld[p]) for p in pairs}
    for it in range(3):
        X = {p: X[p] + _dot(X[p], P[p]) for p in pairs}
        if it < 2:
            P = {p: _dot(P[p], P[p]) for p in pairs}
    M = {p: _dot(X[p], Lo[p]) for p in pairs}
    M2 = {p: _dot(M[p], M[p]) for p in pairs}
    Y = {p: (eye - M[p]) + _dot(eye - M[p], M2[p]) for p in pairs}
    Tm = {p: _dot(Y[p], X[p]) for p in pairs}
    uw = {p: _dot(Tm[p], jnp.concatenate([v[p] * bt[p], kb[p] * jnp.exp(gc[p])], axis=1)) for p in pairs}
    qe = {p: q[p] * jnp.exp(gc[p]) for p in pairs}
    g_last = {p: gc[p][CHUNK - 1:CHUNK] for p in pairs}
    kd = {p: k[p] * jnp.exp(g_last[p] - gc[p]) for p in pairs}

    S = [s_ref[h] for h in range(N_HEADS)]
    o_rows = []
    for c in range(n_chunk):
        v_new = [uw[h, c][:, :HEAD_DIM] - _dot(uw[h, c][:, HEAD_DIM:], S[h]) for h in range(N_HEADS)]
        o_c = [_dot(qe[h, c], S[h]) + _dot(qk[h, c], v_new[h]) for h in range(N_HEADS)]
        S = [S[h] * jnp.exp(g_last[h, c]) + _dot(kd[h, c], v_new[h], TN) for h in range(N_HEADS)]
        o_rows.append(o_c)
    for h in range(N_HEADS):
        s_ref[h] = S[h]
    ng = ng_ref[...]
    z = z_ref[...]
    outs = []
    for h in range(N_HEADS):
        o = jnp.concatenate([o_rows[c][h] for c in range(n_chunk)], axis=0) if n_chunk > 1 else o_rows[0][h]
        o = o * lax.rsqrt(jnp.mean(o * o, axis=-1, keepdims=True) + RMS_EPS) * ng
        outs.append((o * _silu(z[:, h * HEAD_DIM:(h + 1) * HEAD_DIM])).astype(o_ref.dtype))
    o_ref[...] = jnp.concatenate(outs, axis=1)


def _gdn(proj, ab_t, conv_w, a_log, dt_bias, norm_g, B, T):
    tb = _pick(T, (256, 128, 64))
    nt = T // tb
    r8 = tb // SUBLANE
    w3 = 3 * BRANCH_WIDTH
    pad = lambda p: jnp.pad(p.astype(F32), (0, LANE - N_HEADS))
    hrow = jnp.stack([pad(a_log), pad(dt_bias)])
    bc = lambda p: jnp.broadcast_to(p.astype(F32)[:, None], (N_HEADS, LANE))
    hcol = jnp.concatenate([bc(a_log), bc(dt_bias)], axis=1)
    const = lambda shape: pl.BlockSpec(shape, lambda b, t: (0,) * len(shape))
    return pl.pallas_call(
        functools.partial(_gdn_kernel, tb=tb), grid=(B, nt),
        in_specs=[pl.BlockSpec((tb, w3), lambda b, t: (b * nt + t, COL_GDN_QKV // w3)),
                  pl.BlockSpec((SUBLANE, w3),
                               lambda b, t: (jnp.maximum((b * nt + t) * r8 - 1, 0), COL_GDN_QKV // w3)),
                  pl.BlockSpec((tb, BRANCH_WIDTH), lambda b, t: (b * nt + t, COL_GDN_Z // BRANCH_WIDTH)),
                  pl.BlockSpec((tb, LANE), lambda b, t: (b * nt + t, COL_AB // LANE)),
                  pl.BlockSpec((2 * N_HEADS, tb), lambda b, t: (0, b * nt + t)),
                  const((GDN_CONV, w3)), const((2, LANE)), const((N_HEADS, 2 * LANE)), const((1, HEAD_DIM))],
        out_specs=pl.BlockSpec((tb, BRANCH_WIDTH), lambda b, t: (b * nt + t, 0)),
        out_shape=jax.ShapeDtypeStruct((B * T, BRANCH_WIDTH), BF16),
        scratch_shapes=[pltpu.VMEM((N_HEADS, HEAD_DIM, HEAD_DIM), F32)],
        compiler_params=_cparams(("parallel", "arbitrary")), name="gated_deltanet")(
            proj, proj, proj, proj, ab_t, conv_w, hrow, hcol, norm_g.reshape(1, HEAD_DIM))


def _swa_kernel(q_ref, kc_ref, vc_ref, kp_ref, vp_ref, sink_ref, o_ref):
    has_prev = pl.program_id(1) > 0
    half = SWA_HEAD_DIM
    kj = lax.broadcasted_iota(jnp.int32, (2 * SWA_BLOCK, SWA_BLOCK), 0)
    qi = lax.broadcasted_iota(jnp.int32, (2 * SWA_BLOCK, SWA_BLOCK), 1)
    is_cur = kj < SWA_BLOCK
    valid = (is_cur & (qi >= kj)) | (~is_cur & (kj - SWA_BLOCK > qi) & has_prev)
    dist = jnp.where(valid, jnp.where(is_cur, qi - kj, qi - kj + 2 * SWA_BLOCK).astype(F32), jnp.inf)
    low = lax.broadcasted_iota(jnp.int32, (SWA_BLOCK, LANE), 1) < half
    top = lax.broadcasted_iota(jnp.int32, (LANE, SWA_BLOCK), 0) < half

    q_all = (q_ref[...] * (SWA_HEAD_DIM ** -0.5 * LOG2_E)).astype(BF16)
    swap = lambda x: jnp.concatenate([x[:, half:], x[:, :half]], axis=1)
    k2 = jnp.concatenate([kc_ref[...], kp_ref[...]], axis=0).astype(BF16)
    v2 = jnp.concatenate([vc_ref[...], vp_ref[...]], axis=0).astype(BF16)
    k2 = (k2, swap(k2))
    v2 = (v2, swap(v2))
    sinks = sink_ref[...] * LOG2_E
    heads = range(SWA_Q_HEADS)
    group = SWA_Q_HEADS // SWA_KV_HEADS
    slope2 = [2.0 ** (-8.0 * (hq + 1) / SWA_Q_HEADS) * LOG2_E for hq in heads]
    q = [jnp.where(low if hq % 2 == 0 else ~low, q_all[:, (hq // 2) * LANE:(hq // 2 + 1) * LANE], 0) for hq in heads]
    sel = [(hq % 2) ^ (hq // group) for hq in heads]
    s = [_dot(k2[sel[h]], q[h], NT) - slope2[h] * dist for h in heads]
    sink = [sinks[h:h + 1, :] for h in heads]
    m = [jnp.maximum(jnp.max(s[h], axis=0, keepdims=True), sink[h]) for h in heads]
    p = [jnp.exp2(s[h] - m[h]) for h in heads]
    inv = [1.0 / (jnp.sum(p[h], axis=0, keepdims=True) + jnp.exp2(sink[h] - m[h])) for h in heads]
    o = [_dot(v2[sel[h]], p[h] * inv[h], TN) for h in heads]
    outs = [jnp.where(top, o[2 * t], o[2 * t + 1]).T for t in range(SWA_Q_HEADS // 2)]
    o_ref[...] = jnp.concatenate(outs, axis=1).astype(o_ref.dtype)


def _swa(proj, sinks, B, T):
    nb = T // SWA_BLOCK
    qw = SWA_Q_HEADS * SWA_HEAD_DIM
    cur = lambda col: pl.BlockSpec((SWA_BLOCK, LANE), lambda b, n, col=col: (b * nb + n, col))
    prv = lambda col: pl.BlockSpec((SWA_BLOCK, LANE),
                                   lambda b, n, col=col: (b * nb + jnp.maximum(n - 1, 0), col))
    sink_b = jnp.broadcast_to(sinks.astype(F32)[:, None], (SWA_Q_HEADS, LANE))
    return pl.pallas_call(
        _swa_kernel, grid=(B, nb),
        in_specs=[pl.BlockSpec((SWA_BLOCK, qw), lambda b, n: (b * nb + n, COL_SWA_Q // qw)),
                  cur(COL_SWA_K // LANE), cur(COL_SWA_V // LANE), prv(COL_SWA_K // LANE), prv(COL_SWA_V // LANE),
                  pl.BlockSpec((SWA_Q_HEADS, LANE), lambda b, n: (0, 0))],
        out_specs=pl.BlockSpec((SWA_BLOCK, qw), lambda b, n: (b * nb + n, 0)),
        out_shape=jax.ShapeDtypeStruct((B * T, qw), BF16),
        compiler_params=_cparams(("parallel", "arbitrary")), name="swa")(
            proj, proj, proj, proj, proj, sink_b)


SRC_HG, SRC_QKV, SRC_Z, SRC_AB = 0, 4096, 7168, 8192
N_AB = 2 * N_HEADS
SWA_COLS = COL_AB - COL_SWA_Q


def _w_in_kernel(w_ref, o_ref, *, tk):
    n_gates = o_ref.shape[0] - N_PROJ

    def put(dst, src, n):
        o_ref[dst:dst + n, :] = w_ref[src:src + n, :].astype(BF16)

    put(0, SRC_AB + N_AB + SWA_COLS, n_gates)
    put(n_gates + COL_GDN_QKV, SRC_QKV, 3 * BRANCH_WIDTH)
    put(n_gates + COL_HG_Q, SRC_HG, 4 * BRANCH_WIDTH)
    put(n_gates + COL_GDN_Z, SRC_Z, BRANCH_WIDTH)
    put(n_gates + COL_SWA_Q, SRC_AB + N_AB, SWA_COLS)
    put(n_gates + COL_AB, SRC_AB, N_AB)
    o_ref[n_gates + COL_AB + N_AB:, :] = jnp.zeros((N_PROJ - COL_AB - N_AB, tk), BF16)


def _prep_w_in(w):
    L, d, n_src = w.shape
    n_out = N_PROJ + (n_src - SRC_AB - N_AB - SWA_COLS)
    tk = LANE
    return pl.pallas_call(
        functools.partial(_w_in_kernel, tk=tk), grid=(L, d // tk),
        in_specs=[pl.BlockSpec((None, n_src, tk), lambda l, i: (l, 0, i))],
        out_specs=pl.BlockSpec((None, n_out, tk), lambda l, i: (l, 0, i)),
        out_shape=jax.ShapeDtypeStruct((L, n_out, d), BF16),
        compiler_params=_cparams(("parallel", "parallel")), name="w_in_relayout")(jnp.swapaxes(w, 1, 2))


def kernel(x, ln_in_g, ln_in_b, hg_lb_logits, w_in, gdn_conv_w, gdn_a_log, gdn_dt_bias, hg_norm_g,
           gdn_norm_g, swa_sinks, w_branch, w_out, ln1_g, ln1_b, w_gate_up, w_down, ln2_g, ln2_b):
    B, T, D = x.shape
    M = B * T
    depth = w_in.shape[0]
    lb_all = jnp.cumsum(jax.nn.softmax(hg_lb_logits.astype(F32), axis=0), axis=0)
    lb_all = lb_all - lb_all[0]

    w_in16 = _prep_w_in(w_in)
    w_branch16, w_out16 = w_branch.astype(BF16), w_out.astype(BF16)
    w_down16 = w_down.astype(BF16)

    h32, h16 = _layer_norm(x.reshape(M, D), ln_in_g, ln_in_b)
    for l in range(depth):
        n_gates = w_in16.shape[1] - N_PROJ
        gate_logits = _matmul(h16, w_in16, l, BF16, "in_proj_gates", w_is_nk=True, cols=(0, n_gates))
        proj = _matmul(h16, w_in16, l, F32, "in_proj", w_is_nk=True, cols=(n_gates, n_gates + N_PROJ))
        ab_t = proj[:, COL_AB:COL_AB + N_AB].T
        o_a = _hgrn2(proj, lb_all[l], hg_norm_g[l], B, T)
        o_b = _gdn(proj, ab_t, gdn_conv_w[l], gdn_a_log[l], gdn_dt_bias[l], gdn_norm_g[l], B, T)
        o_c = _swa(proj, swa_sinks[l], B, T)
        merged = _merge(o_a, o_b, o_c, w_branch16, l, gate_logits, D)
        mix = _matmul(merged, w_out16, l, BF16, "out_proj")
        h32, h16 = _layer_norm(mix, ln1_g[l], ln1_b[l], res=h32)
        ff = _glu_matmul(h16, w_gate_up, l)
        ff = _matmul(ff, w_down16, l, BF16, "ffn_down")
        h32, h16 = _layer_norm(ff, ln2_g[l], ln2_b[l], res=h32)
    return h32.reshape(B, T, D)
```
